```python
import math
import jax
import jax.numpy as jnp
from jax import lax
import numpy as np

D_MODEL = 1024
BATCH = 8
SEQ = 4096
DEPTH = 2

HEAD_DIM = 64
N_MIXERS = 4
HEADS_PER_MIXER = D_MODEL // (N_MIXERS * HEAD_DIM)
MIX_WIDTH = N_MIXERS * HEADS_PER_MIXER * HEAD_DIM
A_HEADS = HEADS_PER_MIXER
DILATED_BRANCHES = ((128, 1), (512, 4), (2048, 16))
B_Q_HEADS = HEADS_PER_MIXER
B_KV_HEADS = HEADS_PER_MIXER // 2
B_GROUP = B_Q_HEADS // B_KV_HEADS
B_RADIUS = 128
C_HEADS = HEADS_PER_MIXER
C_QK_DIM = HEAD_DIM // 2
C_V_DIM = HEAD_DIM
D_HEADS = HEADS_PER_MIXER
D_Q_RANK = 3 * D_MODEL // 8
D_KV_RANK = D_MODEL // 8
D_NOPE_DIM = HEAD_DIM
D_ROPE_DIM = HEAD_DIM // 2
D_V_DIM = HEAD_DIM
ROPE_THETA = 10000.0
N_ALIBI_HEADS = A_HEADS + B_Q_HEADS + C_HEADS
QUERY_BLOCK = 128
D_FF = ((8 * D_MODEL + 767) // 768) * 256
A_IN = 3 * A_HEADS * HEAD_DIM
B_IN = (B_Q_HEADS + 2 * B_KV_HEADS) * HEAD_DIM
C_IN = C_HEADS * (4 * C_QK_DIM + C_V_DIM)
D_IN = D_Q_RANK + D_KV_RANK + D_ROPE_DIM
IN_WIDTH = A_IN + B_IN + C_IN + D_IN
IN_SPLITS = (A_IN, A_IN + B_IN, A_IN + B_IN + C_IN)
RMS_EPS = 1e-6
NEG_INF = -1e30
LAMBDA_STD = 0.1

kernel_name = 'hybrid_parallel_head_group_encoder'


def _rms_norm(x, g, eps=RMS_EPS):
    xf = x.astype(jnp.float32)
    y = xf * lax.rsqrt(jnp.mean(xf * xf, axis=-1, keepdims=True) + eps)
    return (y * g.astype(jnp.float32)).astype(x.dtype)


def _alibi_slopes():
    j = jnp.arange(1, N_ALIBI_HEADS + 1, dtype=jnp.float32)
    return jnp.exp2(-8.0 * j / N_ALIBI_HEADS)


def _rope_tables(pos):
    half = D_ROPE_DIM // 2
    inv = jnp.power(ROPE_THETA, -jnp.arange(half, dtype=jnp.float32) / half)
    ang = pos[..., None] * inv
    return jnp.cos(ang), jnp.sin(ang)


def _rope(x, cos, sin):
    half = x.shape[-1] // 2
    xf = x.astype(jnp.float32)
    x1, x2 = xf[..., :half], xf[..., half:]
    return jnp.concatenate([x1 * cos - x2 * sin, x1 * sin + x2 * cos], axis=-1).astype(x.dtype)


def _halo_blocks(a, axis, blk, nb):
    T = a.shape[axis]
    pad = [(0, 0)] * a.ndim
    pad[axis] = (blk, nb * blk - T + blk)
    ap = jnp.pad(a, pad)
    ap = ap.reshape(a.shape[:axis] + (nb + 2, blk) + a.shape[axis + 1:])
    prev = lax.slice_in_dim(ap, 0, nb, axis=axis)
    cur = lax.slice_in_dim(ap, 1, nb + 1, axis=axis)
    nxt = lax.slice_in_dim(ap, 2, nb + 2, axis=axis)
    return jnp.concatenate([prev, cur, nxt], axis=axis + 1)


def _banded_attention(q, k, v, pos, slope, radius, sink=None):
    bsz, G, R, T, Dk = q.shape
    W = radius
    nb = -(-T // W)
    qb = jnp.pad(q, ((0, 0), (0, 0), (0, 0), (0, nb * W - T), (0, 0))).reshape(bsz, G, R, nb, W, Dk)
    kb = _halo_blocks(k, 2, W, nb)
    vb = _halo_blocks(v, 2, W, nb)
    qp = jnp.pad(pos, ((0, 0), (0, nb * W - T))).reshape(bsz, nb, W)
    kp = _halo_blocks(pos, 1, W, nb)
    qi = jnp.arange(nb * W).reshape(nb, W)
    ki = (jnp.arange(nb)[:, None] - 1) * W + jnp.arange(3 * W)[None, :]
    valid = ((ki[:, None, :] >= 0) & (ki[:, None, :] < T)
             & (jnp.abs(qi[:, :, None] - ki[:, None, :]) <= radius))
    dist = jnp.abs(qp[:, :, :, None] - kp[:, :, None, :])
    s = jnp.einsum('bgrnqd,bgnkd->bgrnqk', qb, kb, preferred_element_type=jnp.float32)
    s = s - slope.astype(jnp.float32)[None, :, :, None, None, None] * dist[:, None, None]
    s = jnp.where(valid, s, NEG_INF)
    m = jnp.max(s, axis=-1)
    if sink is not None:
        sk = sink.astype(jnp.float32)[None, :, :, None, None]
        m = jnp.maximum(m, sk)
    p = jnp.exp(s - m[..., None])
    den = jnp.sum(p, axis=-1)
    if sink is not None:
        den = den + jnp.exp(sk - m)
    o = jnp.einsum('bgrnqk,bgnkd->bgrnqd', p.astype(v.dtype), vb,
                   preferred_element_type=jnp.float32) / den[..., None]
    lse = m + jnp.log(den)
    o = o.reshape(bsz, G, R, nb * W, -1)[:, :, :, :T]
    lse = lse.reshape(bsz, G, R, nb * W)[..., :T]
    return o, lse


def _dilated_mixture(q, k, v, pos, slopes):
    bsz, H, T, Dh = q.shape
    outs, lses = [], []
    for window, d in DILATED_BRANCHES:
        n = T // d

        def fold(a, d=d, n=n):
            return a.reshape(bsz, H, n, d, -1).transpose(0, 3, 1, 2, 4).reshape(bsz * d, H, n, a.shape[-1])

        pf = pos.reshape(bsz, n, d).transpose(0, 2, 1).reshape(bsz * d, n)
        o, lse = _banded_attention(fold(q)[:, :, None], fold(k), fold(v), pf,
                                   slopes[:, None], window // (2 * d))
        outs.append(o[:, :, 0].reshape(bsz, d, H, n, Dh).transpose(0, 2, 3, 1, 4).reshape(bsz, H, T, Dh))
        lses.append(lse[:, :, 0].reshape(bsz, d, H, n).transpose(0, 2, 3, 1).reshape(bsz, H, T))
    wts = jax.nn.softmax(jnp.stack(lses), axis=0)
    return jnp.einsum('ibht,ibhtd->bhtd', wts, jnp.stack(outs)).astype(q.dtype)


def _query_blocks(a):
    bsz, H, T, D = a.shape
    return a.reshape(bsz, H, T // QUERY_BLOCK, QUERY_BLOCK, D).transpose(2, 0, 1, 3, 4)


def _merge_query_blocks(o):
    nq, bsz, H, Q, D = o.shape
    return o.transpose(1, 2, 0, 3, 4).reshape(bsz, H, nq * Q, D)


def _diff_attention(q1, q2, k1, k2, v, pos, slopes, lam):
    bsz, H, T, _ = q1.shape
    pb = pos.reshape(bsz, T // QUERY_BLOCK, QUERY_BLOCK).transpose(1, 0, 2)

    def block(args):
        q1b, q2b, pq = args
        bias = -slopes[None, :, None, None] * jnp.abs(pq[:, None, :, None] - pos[:, None, None, :])
        a1 = jax.nn.softmax(jnp.einsum('bhqd,bhkd->bhqk', q1b, k1, preferred_element_type=jnp.float32) + bias, axis=-1)
        a2 = jax.nn.softmax(jnp.einsum('bhqd,bhkd->bhqk', q2b, k2, preferred_element_type=jnp.float32) + bias, axis=-1)
        return jnp.einsum('bhqk,bhkd->bhqd', (a1 - lam * a2).astype(v.dtype), v,
                          preferred_element_type=jnp.float32)

    return _merge_query_blocks(lax.map(block, (_query_blocks(q1), _query_blocks(q2), pb)))


def _mla_attention(q_nope, q_rope, k_nope, k_rope, v):
    def block(args):
        qn, qr = args
        s = (jnp.einsum('bhqd,bhkd->bhqk', qn, k_nope, preferred_element_type=jnp.float32)
             + jnp.einsum('bhqd,bkd->bhqk', qr, k_rope, preferred_element_type=jnp.float32))
        p = jax.nn.softmax(s, axis=-1)
        return jnp.einsum('bhqk,bhkd->bhqd', p.astype(v.dtype), v, preferred_element_type=jnp.float32)

    return _merge_query_blocks(lax.map(block, (_query_blocks(q_nope), _query_blocks(q_rope))))


def _hybrid_token_mixer(h, pos, cos, sin, layer, w_in, sink, lq1, lk1, lq2, lk2,
                        g_diff, g_q, g_kv, w_uq, w_ukv, w_out):
    bsz, T, _ = h.shape
    slopes = _alibi_slopes()
    sl_b = slopes[:B_Q_HEADS]
    sl_c = slopes[B_Q_HEADS:B_Q_HEADS + C_HEADS]
    sl_a = slopes[B_Q_HEADS + C_HEADS:]
    z = jnp.einsum('btd,de->bte', h, w_in)
    z_a, z_b, z_c, z_d = jnp.split(z, IN_SPLITS, axis=-1)

    qa, ka, va = [t.reshape(bsz, T, A_HEADS, HEAD_DIM).transpose(0, 2, 1, 3)
                  for t in jnp.split(z_a, 3, axis=-1)]
    o_a = _dilated_mixture(qa * HEAD_DIM ** -0.5, ka, va, pos, sl_a)
    o_a = o_a.transpose(0, 2, 1, 3).reshape(bsz, T, A_HEADS * HEAD_DIM)

    qb, kb, vb = jnp.split(z_b, (B_Q_HEADS * HEAD_DIM, (B_Q_HEADS + B_KV_HEADS) * HEAD_DIM), axis=-1)
    qb = qb.reshape(bsz, T, B_KV_HEADS, B_GROUP, HEAD_DIM).transpose(0, 2, 3, 1, 4) * HEAD_DIM ** -0.5
    kb = kb.reshape(bsz, T, B_KV_HEADS, HEAD_DIM).transpose(0, 2, 1, 3)
    vb = vb.reshape(bsz, T, B_KV_HEADS, HEAD_DIM).transpose(0, 2, 1, 3)
    o_b, _ = _banded_attention(qb, kb, vb, pos, sl_b.reshape(B_KV_HEADS, B_GROUP), B_RADIUS,
                               sink.reshape(B_KV_HEADS, B_GROUP))
    o_b = o_b.transpose(0, 3, 1, 2, 4).reshape(bsz, T, B_Q_HEADS * HEAD_DIM).astype(h.dtype)

    qc, kc, vc = jnp.split(z_c, (C_HEADS * 2 * C_QK_DIM, C_HEADS * 4 * C_QK_DIM), axis=-1)
    qc = qc.reshape(bsz, T, C_HEADS, 2, C_QK_DIM).transpose(3, 0, 2, 1, 4) * C_QK_DIM ** -0.5
    kc = kc.reshape(bsz, T, C_HEADS, 2, C_QK_DIM).transpose(3, 0, 2, 1, 4)
    vc = vc.reshape(bsz, T, C_HEADS, C_V_DIM).transpose(0, 2, 1, 3)
    lam_init = 0.8 - 0.6 * math.exp(-0.3 * layer)
    f32 = jnp.float32
    lam = (jnp.exp(jnp.sum(lq1.astype(f32) * lk1.astype(f32)))
           - jnp.exp(jnp.sum(lq2.astype(f32) * lk2.astype(f32))) + lam_init)
    o_c = _diff_attention(qc[0], qc[1], kc[0], kc[1], vc, pos, sl_c, lam)
    o_c = _rms_norm(o_c, g_diff) * (1.0 - lam_init)
    o_c = o_c.transpose(0, 2, 1, 3).reshape(bsz, T, C_HEADS * C_V_DIM).astype(h.dtype)

    cq, ckv, kr = jnp.split(z_d, (D_Q_RANK, D_Q_RANK + D_KV_RANK), axis=-1)
    qd = jnp.einsum('btr,re->bte', _rms_norm(cq, g_q), w_uq).reshape(bsz, T, D_HEADS, D_NOPE_DIM + D_ROPE_DIM)
    q_nope = qd[..., :D_NOPE_DIM]
    q_rope = _rope(qd[..., D_NOPE_DIM:], cos[:, :, None], sin[:, :, None])
    kv = jnp.einsum('btr,re->bte', _rms_norm(ckv, g_kv), w_ukv).reshape(bsz, T, D_HEADS, D_NOPE_DIM + D_V_DIM)
    k_nope, v_d = kv[..., :D_NOPE_DIM], kv[..., D_NOPE_DIM:]
    k_rope = _rope(kr, cos, sin)
    d_scale = (D_NOPE_DIM + D_ROPE_DIM) ** -0.5
    o_d = _mla_attention(q_nope.transpose(0, 2, 1, 3) * d_scale, q_rope.transpose(0, 2, 1, 3) * d_scale,
                         k_nope.transpose(0, 2, 1, 3), k_rope, v_d.transpose(0, 2, 1, 3))
    o_d = o_d.transpose(0, 2, 1, 3).reshape(bsz, T, D_HEADS * D_V_DIM).astype(h.dtype)

    mix = jnp.concatenate([o_a, o_b, o_c, o_d], axis=-1)
    return jnp.einsum('btm,md->btd', mix, w_out)


def _swiglu(h, w_gate_up, w_down):
    g, u = jnp.split(jnp.einsum('btd,df->btf', h, w_gate_up), 2, axis=-1)
    return jnp.einsum('btf,fd->btd', jax.nn.silu(g) * u, w_down)


def setup_inputs(seed: int = 0) -> dict:
    key = jax.random.key(seed)
    ks = jax.random.split(key, 23)
    L, D = DEPTH, D_MODEL
    f32 = jnp.float32

    def dense(k, shape, fan_in, gain=1.0):
        return gain * fan_in ** -0.5 * jax.random.normal(k, shape, f32)

    def norm_gain(k, n):
        return 1.0 + 0.05 * jax.random.normal(k, (L, n), f32)

    return {
        'x': jax.random.normal(ks[0], (BATCH, SEQ, D), f32),
        'c': jax.random.normal(ks[1], (BATCH, D), f32),
        'positions': (jax.random.randint(ks[2], (BATCH, 1), 0, SEQ, dtype=jnp.int32)
                      + jnp.arange(SEQ, dtype=jnp.int32)[None, :]),
        'w_ada': dense(ks[3], (L, D, 6 * D), D, 0.5),
        'b_ada': 0.01 * jax.random.normal(ks[4], (L, 6 * D), f32),
        'g_pre_mix': norm_gain(ks[5], D),
        'g_post_mix': norm_gain(ks[6], D),
        'w_in': dense(ks[7], (L, D, IN_WIDTH), D),
        'sink_logits': 0.5 * jax.random.normal(ks[8], (L, B_Q_HEADS), f32),
        'lam_q1': LAMBDA_STD * jax.random.normal(ks[9], (L, C_QK_DIM), f32),
        'lam_k1': LAMBDA_STD * jax.random.normal(ks[10], (L, C_QK_DIM), f32),
        'lam_q2': LAMBDA_STD * jax.random.normal(ks[11], (L, C_QK_DIM), f32),
        'lam_k2': LAMBDA_STD * jax.random.normal(ks[12], (L, C_QK_DIM), f32),
        'g_diff': norm_gain(ks[13], C_V_DIM),
        'g_mla_q': norm_gain(ks[14], D_Q_RANK),
        'g_mla_kv': norm_gain(ks[15], D_KV_RANK),
        'w_uq': dense(ks[16], (L, D_Q_RANK, D_HEADS * (D_NOPE_DIM + D_ROPE_DIM)), D_Q_RANK),
        'w_ukv': dense(ks[17], (L, D_KV_RANK, D_HEADS * (D_NOPE_DIM + D_V_DIM)), D_KV_RANK),
        'w_out': dense(ks[18], (L, MIX_WIDTH, D), MIX_WIDTH),
        'g_pre_ffn': norm_gain(ks[19], D),
        'g_post_ffn': norm_gain(ks[20], D),
        'w_gate_up': dense(ks[21], (L, D, 2 * D_FF), D),
        'w_down': dense(ks[22], (L, D_FF, D), D_FF),
    }


def reference(x, c, positions, w_ada, b_ada, g_pre_mix, g_post_mix, w_in, sink_logits,
              lam_q1, lam_k1, lam_q2, lam_k2, g_diff, g_mla_q, g_mla_kv, w_uq, w_ukv,
              w_out, g_pre_ffn, g_post_ffn, w_gate_up, w_down):
    pos = positions.astype(jnp.float32)
    cos, sin = _rope_tables(pos)
    c_act = jax.nn.silu(c)
    for layer in range(DEPTH):
        mod = jnp.einsum('bd,de->be', c_act, w_ada[layer]) + b_ada[layer]
        sh_m, sc_m, gt_m, sh_f, sc_f, gt_f = [m[:, None, :] for m in jnp.split(mod, 6, axis=-1)]
        h = _rms_norm(x, g_pre_mix[layer]) * (1.0 + sc_m) + sh_m
        mix = _hybrid_token_mixer(h, pos, cos, sin, layer, w_in[layer], sink_logits[layer],
                                  lam_q1[layer], lam_k1[layer], lam_q2[layer], lam_k2[layer],
                                  g_diff[layer], g_mla_q[layer], g_mla_kv[layer],
                                  w_uq[layer], w_ukv[layer], w_out[layer])
        x = x + gt_m * _rms_norm(mix, g_post_mix[layer])
        h = _rms_norm(x, g_pre_ffn[layer]) * (1.0 + sc_f) + sh_f
        x = x + gt_f * _rms_norm(_swiglu(h, w_gate_up[layer], w_down[layer]), g_post_ffn[layer])
    return x
```

```python
import functools
import math

import numpy as np
import jax
import jax.numpy as jnp
from jax import lax
from jax.experimental import pallas as pl
from jax.experimental.pallas import tpu as pltpu

F32 = jnp.float32
BF16 = jnp.bfloat16

D_MODEL = 1024
HEAD_DIM = 64
N_HEADS = 4
GROUP_W = N_HEADS * HEAD_DIM
DILATIONS = (1, 4, 16)
A_RADIUS = 64
B_RADIUS = 128
C_QK = 32
D_Q_RANK = 384
D_KV_RANK = 128
D_ROPE = 32
D_FF = 2816
ROPE_THETA = 10000.0
N_ALIBI = 12
EPS = 1e-6
NEG = -1e30
LANE = 128

RAW_D = 768
RAW_PASS = 2176
RAW_W = RAW_D + RAW_PASS
Z_QD, Z_KD, Z_VD = 0, 512, 1024
Z_QA, Z_KA, Z_VA = 1280, 1536, 1792
Z_QB, Z_VB = 2048, 2304
Z_QC, Z_KC, Z_VC = 2560, 2816, 3072
Z_KB = 3328
Z_PAD = 3456
Z_W = 3584

TM_PROJ = 512
TQ_DENSE = 256
TK_DENSE = 512
BAND_BLK = 128
VMEM_LIMIT = 56 * 1024 * 1024


def _cparams(sem):
    return pltpu.CompilerParams(dimension_semantics=sem, vmem_limit_bytes=VMEM_LIMIT)


def _rms(x, g):
    ms = jnp.mean(x * x, axis=-1, keepdims=True)
    return x * lax.rsqrt(ms + EPS) * g


def _dot(a, b):
    return jnp.dot(a, b, preferred_element_type=F32)


def _dot_nt(a, b):
    return lax.dot_general(a, b, (((1,), (1,)), ((), ())), preferred_element_type=F32)


def _lane_iota(w):
    return lax.broadcasted_iota(jnp.int32, (1, w), 1)


def _lane_mask(w, lo, hi):
    lane = _lane_iota(w)
    return (lane >= lo) & (lane < hi)


def _ada_kernel(c_ref, w_ref, b_ref, o_ref):
    c = c_ref[...]
    c_act = c / (1.0 + jnp.exp(-c))
    o_ref[0] = jnp.dot(c_act, w_ref[0], precision=lax.Precision.HIGHEST,
                       preferred_element_type=F32) + b_ref[0]


def _ada_modulation(c, w_ada, b_ada):
    n_layers, d, six_d = w_ada.shape
    bsz = c.shape[0]
    tn = 1024
    return pl.pallas_call(
        _ada_kernel,
        grid=(n_layers, six_d // tn),
        in_specs=[pl.BlockSpec((bsz, d), lambda l, j: (0, 0)),
                  pl.BlockSpec((1, d, tn), lambda l, j: (l, 0, j)),
                  pl.BlockSpec((1, 1, tn), lambda l, j: (l, 0, j))],
        out_specs=pl.BlockSpec((1, bsz, tn), lambda l, j: (l, 0, j)),
        out_shape=jax.ShapeDtypeStruct((n_layers, bsz, six_d), F32),
        compiler_params=_cparams(("arbitrary", "arbitrary")),
        name="ada_modulation",
    )(c, w_ada, b_ada.reshape(n_layers, 1, six_d))


def _rope_kernel(p_ref, inv_ref, cos_ref, sin_ref):
    ang = p_ref[0] * inv_ref[...]
    cos_ref[0] = jnp.cos(ang)
    sin_ref[0] = jnp.sin(ang)


def _rope_tables(pos_col, inv_full):
    bsz, t, _ = pos_col.shape
    tm = 1024
    spec = pl.BlockSpec((1, tm, LANE), lambda b, i: (b, i, 0))
    return pl.pallas_call(
        _rope_kernel,
        grid=(bsz, t // tm),
        in_specs=[pl.BlockSpec((1, tm, 1), lambda b, i: (b, i, 0)),
                  pl.BlockSpec((1, LANE), lambda b, i: (0, 0))],
        out_specs=[spec, spec],
        out_shape=[jax.ShapeDtypeStruct((bsz, t, LANE), F32)] * 2,
        compiler_params=_cparams(("arbitrary", "arbitrary")),
        name="rope_tables",
    )(pos_col, inv_full)


def _inproj_kernel(x_ref, mod_ref, g_ref, w1_ref, cs_ref, gq_ref, gkv_ref, wq2_ref, wkv2_ref,
                   cos_ref, sin_ref, z_ref, *, d_scale):
    x = x_ref[0]
    mod = mod_ref[0]
    shift, scale = mod[:, 0:D_MODEL], mod[:, D_MODEL:2 * D_MODEL]
    h = (_rms(x, g_ref[...]) * (1.0 + scale) + shift).astype(BF16)
    raw = _dot(h, w1_ref[...])
    cq = raw[:, 0:D_Q_RANK]
    ckv = raw[:, D_Q_RANK:D_Q_RANK + D_KV_RANK]
    kr_a = raw[:, 512:640]
    kr_b = raw[:, 640:768]
    cos = cos_ref[0]
    sin = sin_ref[0]
    cos4 = jnp.concatenate([cos] * N_HEADS, axis=1)
    sin4 = jnp.concatenate([sin] * N_HEADS, axis=1)
    q2 = _dot(_rms(cq, gq_ref[...]).astype(BF16), wq2_ref[...])
    qd = (q2[:, 0:512] * cos4 + q2[:, 512:1024] * sin4) * d_scale
    kv2 = _dot(_rms(ckv, gkv_ref[...]).astype(BF16), wkv2_ref[...])
    kr = kr_a * cos + kr_b * sin
    kd = kv2[:, 0:512] + jnp.concatenate([kr] * N_HEADS, axis=1)
    z_ref[0, :, Z_QD:Z_QD + 512] = qd.astype(BF16)
    z_ref[0, :, Z_KD:Z_KD + 512] = kd.astype(BF16)
    z_ref[0, :, Z_VD:Z_VD + 256] = kv2[:, 512:768].astype(BF16)
    z_ref[0, :, Z_QA:Z_PAD] = (raw[:, RAW_D:RAW_W] * cs_ref[...]).astype(BF16)
    z_ref[0, :, Z_PAD:Z_W] = jnp.zeros((x.shape[0], Z_W - Z_PAD), BF16)


def _input_projection(x, mod, g_pre, w1, col_scale, g_q, g_kv, wq2, wkv2, cos_t, sin_t, layer):
    bsz, t, d = x.shape
    tm = TM_PROJ
    const = lambda b, i: (0, 0)
    lsel = lambda b, i: (layer, 0, 0)
    kern = functools.partial(_inproj_kernel, d_scale=(HEAD_DIM + D_ROPE) ** -0.5)
    return pl.pallas_call(
        kern,
        grid=(bsz, t // tm),
        in_specs=[pl.BlockSpec((1, tm, d), lambda b, i: (b, i, 0)),
                  pl.BlockSpec((None, 1, 1, 6 * d), lambda b, i: (layer, b, 0, 0)),
                  pl.BlockSpec((None, 1, d), lsel),
                  pl.BlockSpec((None, d, RAW_W), lsel),
                  pl.BlockSpec((1, RAW_PASS), const),
                  pl.BlockSpec((None, 1, D_Q_RANK), lsel),
                  pl.BlockSpec((None, 1, D_KV_RANK), lsel),
                  pl.BlockSpec((None, D_Q_RANK, 1024), lsel),
                  pl.BlockSpec((None, D_KV_RANK, 768), lsel),
                  pl.BlockSpec((1, tm, LANE), lambda b, i: (b, i, 0)),
                  pl.BlockSpec((1, tm, LANE), lambda b, i: (b, i, 0))],
        out_specs=pl.BlockSpec((1, tm, Z_W), lambda b, i: (b, i, 0)),
        out_shape=jax.ShapeDtypeStruct((bsz, t, Z_W), BF16),
        compiler_params=_cparams(("arbitrary", "arbitrary")),
        name="input_projection",
    )(x, mod, g_pre, w1, col_scale, g_q, g_kv, wq2, wkv2, cos_t, sin_t)


def _banded_kernel(sl_ref, sink_ref, q_ref, k_ref, v_ref, pc_ref, pr_ref, *rest,
                   n, radius, groups, with_lse):
    if with_lse:
        o_ref, lse_ref, vm_ref = rest
    else:
        o_ref, vm_ref = rest
        lse_ref = None
    blk = BAND_BLK
    win = min(3 * blk, n)
    n_g = len(groups)
    v_all = v_ref[...]
    lane_o = _lane_iota(GROUP_W)
    for g, grp in enumerate(groups):
        vmask = (lane_o >= grp[4]) & (lane_o < grp[5])
        vm_ref[g] = jnp.where(vmask, v_all, jnp.zeros_like(v_all))

    def body(i, carry):
        r0 = pl.multiple_of(i * blk, blk)
        start = pl.multiple_of(jnp.clip(r0 - blk, 0, n - win), blk)
        sblk = start // blk
        q = q_ref[pl.ds(r0, blk), :]
        parts = []
        for (q_lo, q_w, qm_lo, qm_hi, _, _, _, _) in groups:
            qg = q[:, q_lo:q_lo + q_w]
            parts.append(jnp.where(_lane_mask(q_w, qm_lo, qm_hi), qg, jnp.zeros_like(qg)))
        qs = jnp.concatenate(parts, axis=0)
        kw = k_ref[pl.ds(start, win), :]
        s = _dot_nt(qs, kw)
        pq = pc_ref[pl.ds(r0, blk), :]
        pk = jnp.concatenate([pr_ref[sblk + j] for j in range(win // blk)], axis=1)
        dist = jnp.abs(pq - pk)
        qi = r0 + lax.broadcasted_iota(jnp.int32, (blk, 1), 0)
        ki = start + lax.broadcasted_iota(jnp.int32, (1, win), 1)
        valid = jnp.abs(qi - ki) <= radius
        p_parts, inv_parts, lse_parts = [], [], []
        for g, grp in enumerate(groups):
            sg = s[g * blk:(g + 1) * blk] - sl_ref[grp[6]] * dist
            sg = jnp.where(valid, sg, NEG)
            m = jnp.max(sg, axis=1, keepdims=True)
            if grp[7] is not None:
                sk = sink_ref[grp[7]]
                m = jnp.maximum(m, sk)
            p = jnp.exp(sg - m)
            den = jnp.sum(p, axis=1, keepdims=True)
            if grp[7] is not None:
                den = den + jnp.exp(sk - m)
            p_parts.append(p.astype(BF16))
            inv_parts.append(1.0 / den)
            lse_parts.append(m + jnp.log(den))
        pcat = jnp.concatenate(p_parts, axis=1)
        vcat = jnp.concatenate([vm_ref[g, pl.ds(start, win), :] for g in range(n_g)], axis=0)
        o = _dot(pcat, vcat)
        inv_full = jnp.zeros((blk, GROUP_W), F32)
        lse_full = jnp.zeros((blk, GROUP_W), F32)
        for g, grp in enumerate(groups):
            vmask = (lane_o >= grp[4]) & (lane_o < grp[5])
            inv_full = jnp.where(vmask, inv_parts[g], inv_full)
            lse_full = jnp.where(vmask, lse_parts[g], lse_full)
        o_ref[pl.ds(r0, blk), :] = (o * inv_full).astype(o_ref.dtype)
        if with_lse:
            lse_ref[pl.ds(r0, blk), :] = lse_full
        return carry

    lax.fori_loop(0, n // blk, body, 0)


def _banded_attention(slopes, sink, z, pos, *, dil, radius, groups, q_col, q_w, k_col, k_w, v_col,
                      with_lse, out_dtype, name):
    bsz, t, zw = z.shape
    n = t // dil
    blk = BAND_BLK
    zf = z.reshape(bsz, n, dil * zw)
    pos_f = pos.reshape(bsz, n, dil).transpose(0, 2, 1)
    pos_col = pos_f.reshape(bsz, dil, n, 1)
    pos_row = pos_f.reshape(bsz, dil, n // blk, 1, blk)
    kern = functools.partial(_banded_kernel, n=n, radius=radius, groups=groups, with_lse=with_lse)
    o_spec = pl.BlockSpec((None, n, GROUP_W), lambda b, r: (b, 0, r))
    out_specs = [o_spec]
    out_shape = [jax.ShapeDtypeStruct((bsz, n, dil * GROUP_W), out_dtype)]
    if with_lse:
        out_specs.append(o_spec)
        out_shape.append(jax.ShapeDtypeStruct((bsz, n, dil * GROUP_W), F32))
    smem = pl.BlockSpec(memory_space=pltpu.SMEM)

    def zspec(col, w):
        return pl.BlockSpec((None, n, w), lambda b, r: (b, 0, r * (zw // w) + col // w))

    outs = pl.pallas_call(
        kern,
        grid=(bsz, dil),
        in_specs=[smem, smem, zspec(q_col, q_w), zspec(k_col, k_w), zspec(v_col, GROUP_W),
                  pl.BlockSpec((None, None, n, 1), lambda b, r: (b, r, 0, 0)),
                  pl.BlockSpec((None, None, n // blk, 1, blk), lambda b, r: (b, r, 0, 0, 0))],
        out_specs=out_specs,
        out_shape=out_shape,
        scratch_shapes=[pltpu.VMEM((len(groups), n, GROUP_W), BF16)],
        compiler_params=_cparams(("arbitrary", "arbitrary")),
        name=name,
    )(slopes, sink, zf, zf, zf, pos_col, pos_row)
    return [o.reshape(bsz, t, GROUP_W) for o in outs]


def _flash_step(s_groups, m, l, acc, v_stack, n_maps, tq):
    n_g = len(s_groups)
    n_heads = n_g // n_maps
    m_new, l_new, alphas, p_parts = [], [], [], []
    for g in range(n_g):
        mb = jnp.max(s_groups[g], axis=1, keepdims=True)
        mn = jnp.maximum(m[g], mb)
        alpha = jnp.exp(m[g] - mn)
        p = jnp.exp(s_groups[g] - mn)
        m_new.append(mn)
        l_new.append(alpha * l[g] + jnp.sum(p, axis=1, keepdims=True))
        alphas.append(alpha)
        p_parts.append(p.astype(BF16))
    rows = []
    for mp in range(n_maps):
        rows.append(jnp.concatenate([p_parts[h * n_maps + mp] for h in range(n_heads)], axis=1))
    p_all = jnp.concatenate(rows, axis=0) if n_maps > 1 else rows[0]
    pv = _dot(p_all, v_stack)
    lane = _lane_iota(LANE)
    a_rows = []
    for mp in range(n_maps):
        a_rows.append(jnp.where(lane < HEAD_DIM, alphas[mp], alphas[n_maps + mp]))
    a_full = jnp.concatenate(a_rows, axis=0) if n_maps > 1 else a_rows[0]
    return m_new, l_new, a_full * acc + pv


def _fill_masked_v(v_ref, vm_ref):
    v = v_ref[0]
    lane = _lane_iota(LANE)
    vm_ref[0] = jnp.where(lane < HEAD_DIM, v, jnp.zeros_like(v))
    vm_ref[1] = jnp.where(lane >= HEAD_DIM, v, jnp.zeros_like(v))


def _diff_kernel(sl_ref, lq1_ref, lk1_ref, lq2_ref, lk2_ref, gd_ref, q_ref, k_ref, v_ref, pc_ref,
                 pr_ref, o_ref, vm_ref, *, t, lam_init):
    tq, tk = TQ_DENSE, TK_DENSE
    hp = pl.program_id(1)

    @pl.when(pl.program_id(2) == 0)
    def _():
        _fill_masked_v(v_ref, vm_ref)

    q = q_ref[0]
    lane = _lane_iota(LANE)
    qs = jnp.concatenate(
        [jnp.where((lane >= C_QK * g) & (lane < C_QK * (g + 1)), q, jnp.zeros_like(q))
         for g in range(4)], axis=0)
    pq = pc_ref[0]
    slope0 = sl_ref[N_HEADS + 2 * hp]
    slope1 = sl_ref[N_HEADS + 2 * hp + 1]

    def body(kb, carry):
        m, l, acc = carry
        off = pl.multiple_of(kb * tk, tk)
        s = _dot_nt(qs, k_ref[0, pl.ds(off, tk), :])
        dist = jnp.abs(pq - pr_ref[0, kb])
        b0 = slope0 * dist
        b1 = slope1 * dist
        s_groups = [s[0:tq] - b0, s[tq:2 * tq] - b0, s[2 * tq:3 * tq] - b1, s[3 * tq:4 * tq] - b1]
        v_stack = jnp.concatenate([vm_ref[0, pl.ds(off, tk), :], vm_ref[1, pl.ds(off, tk), :]], axis=0)
        return _flash_step(s_groups, m, l, acc, v_stack, 2, tq)

    init = ([jnp.full((tq, 1), NEG, F32)] * 4, [jnp.zeros((tq, 1), F32)] * 4,
            jnp.zeros((2 * tq, LANE), F32))
    m, l, acc = lax.fori_loop(0, t // tk, body, init)
    lam = (jnp.exp(jnp.sum(lq1_ref[...] * lk1_ref[...], axis=1, keepdims=True))
           - jnp.exp(jnp.sum(lq2_ref[...] * lk2_ref[...], axis=1, keepdims=True)) + lam_init)
    o1 = acc[0:tq] * jnp.where(lane < HEAD_DIM, 1.0 / l[0], 1.0 / l[2])
    o2 = acc[tq:2 * tq] * jnp.where(lane < HEAD_DIM, 1.0 / l[1], 1.0 / l[3])
    o = o1 - lam * o2
    o_sq = o * o
    ms0 = jnp.sum(jnp.where(lane < HEAD_DIM, o_sq, 0.0), axis=1, keepdims=True) * (1.0 / HEAD_DIM)
    ms1 = jnp.sum(jnp.where(lane >= HEAD_DIM, o_sq, 0.0), axis=1, keepdims=True) * (1.0 / HEAD_DIM)
    ms = jnp.where(lane < HEAD_DIM, ms0, ms1)
    y = o * lax.rsqrt(ms + EPS) * gd_ref[...]
    o_ref[0] = (y * (1.0 - lam_init)).astype(o_ref.dtype)


def _diff_attention(slopes, lq1, lk1, lq2, lk2, g_diff2, z, pos_col, pos_rows, layer):
    bsz, t, _ = z.shape
    tq, tk = TQ_DENSE, TK_DENSE
    lam_init = 0.8 - 0.6 * math.exp(-0.3 * layer)
    kern = functools.partial(_diff_kernel, t=t, lam_init=lam_init)
    lsel = lambda b, h, i: (layer, 0, 0)
    return pl.pallas_call(
        kern,
        grid=(bsz, 2, t // tq),
        in_specs=[pl.BlockSpec(memory_space=pltpu.SMEM),
                  pl.BlockSpec((None, 1, C_QK), lsel), pl.BlockSpec((None, 1, C_QK), lsel),
                  pl.BlockSpec((None, 1, C_QK), lsel), pl.BlockSpec((None, 1, C_QK), lsel),
                  pl.BlockSpec((None, 1, LANE), lsel),
                  pl.BlockSpec((1, tq, LANE), lambda b, h, i: (b, i, Z_QC // LANE + h)),
                  pl.BlockSpec((1, t, LANE), lambda b, h, i: (b, 0, Z_KC // LANE + h)),
                  pl.BlockSpec((1, t, LANE), lambda b, h, i: (b, 0, Z_VC // LANE + h)),
                  pl.BlockSpec((1, tq, 1), lambda b, h, i: (b, i, 0)),
                  pl.BlockSpec((1, t // tk, 1, tk), lambda b, h, i: (b, 0, 0, 0))],
        out_specs=pl.BlockSpec((1, tq, LANE), lambda b, h, i: (b, i, h)),
        out_shape=jax.ShapeDtypeStruct((bsz, t, GROUP_W), BF16),
        scratch_shapes=[pltpu.VMEM((2, t, LANE), BF16)],
        compiler_params=_cparams(("arbitrary", "arbitrary", "arbitrary")),
        name="diff_attention",
    )(slopes, lq1, lk1, lq2, lk2, g_diff2, z, z, z, pos_col, pos_rows)


def _mla_kernel(q_ref, k_ref, v_ref, o_ref, vm_ref, *, t):
    tq, tk = TQ_DENSE, TK_DENSE

    @pl.when(pl.program_id(2) == 0)
    def _():
        _fill_masked_v(v_ref, vm_ref)

    q = q_ref[0]
    q0, q1 = q[:, 0:LANE], q[:, LANE:2 * LANE]

    def body(kb, carry):
        m, l, acc = carry
        off = pl.multiple_of(kb * tk, tk)
        kk = k_ref[0, pl.ds(off, tk), :]
        s_groups = [_dot_nt(q0, kk[:, 0:LANE]), _dot_nt(q1, kk[:, LANE:2 * LANE])]
        v_stack = jnp.concatenate([vm_ref[0, pl.ds(off, tk), :], vm_ref[1, pl.ds(off, tk), :]], axis=0)
        return _flash_step(s_groups, m, l, acc, v_stack, 1, tq)

    init = ([jnp.full((tq, 1), NEG, F32)] * 2, [jnp.zeros((tq, 1), F32)] * 2,
            jnp.zeros((tq, LANE), F32))
    m, l, acc = lax.fori_loop(0, t // tk, body, init)
    lane = _lane_iota(LANE)
    o_ref[0] = (acc * jnp.where(lane < HEAD_DIM, 1.0 / l[0], 1.0 / l[1])).astype(o_ref.dtype)


def _mla_attention(z):
    bsz, t, _ = z.shape
    tq = TQ_DENSE
    kern = functools.partial(_mla_kernel, t=t)
    return pl.pallas_call(
        kern,
        grid=(bsz, 2, t // tq),
        in_specs=[pl.BlockSpec((1, tq, 2 * LANE), lambda b, h, i: (b, i, Z_QD // 256 + h)),
                  pl.BlockSpec((1, t, 2 * LANE), lambda b, h, i: (b, 0, Z_KD // 256 + h)),
                  pl.BlockSpec((1, t, LANE), lambda b, h, i: (b, 0, Z_VD // LANE + h))],
        out_specs=pl.BlockSpec((1, tq, LANE), lambda b, h, i: (b, i, h)),
        out_shape=jax.ShapeDtypeStruct((bsz, t, GROUP_W), BF16),
        scratch_shapes=[pltpu.VMEM((2, t, LANE), BF16)],
        compiler_params=_cparams(("arbitrary", "arbitrary", "arbitrary")),
        name="mla_attention",
    )(z, z, z)


def _mix_ffn_kernel(x_ref, oa1_ref, oa2_ref, oa3_ref, la1_ref, la2_ref, la3_ref, ob_ref, oc_ref, od_ref,
                    mod_ref, gpm_ref, gpf_ref, gqf_ref, wout_ref, wgu_ref, wd_ref, out_ref, a_ref):
    d = D_MODEL
    x = x_ref[0]
    mod = mod_ref[0]
    gate_m = mod[:, 2 * d:3 * d]
    shift_f, scale_f, gate_f = mod[:, 3 * d:4 * d], mod[:, 4 * d:5 * d], mod[:, 5 * d:6 * d]
    l1, l2, l3 = la1_ref[0], la2_ref[0], la3_ref[0]
    mx = jnp.maximum(jnp.maximum(l1, l2), l3)
    e1, e2, e3 = jnp.exp(l1 - mx), jnp.exp(l2 - mx), jnp.exp(l3 - mx)
    mix_a = (e1 * oa1_ref[0] + e2 * oa2_ref[0] + e3 * oa3_ref[0]) / (e1 + e2 + e3)
    mix = jnp.concatenate([mix_a.astype(BF16), ob_ref[0], oc_ref[0], od_ref[0]], axis=1)
    y = _dot(mix, wout_ref[...])
    x1 = x + gate_m * _rms(y, gpm_ref[...])
    h = (_rms(x1, gpf_ref[...]) * (1.0 + scale_f) + shift_f).astype(BF16)
    ch = 256
    for c0 in range(0, D_FF, ch):
        g = _dot(h, wgu_ref[:, c0:c0 + ch])
        u = _dot(h, wgu_ref[:, D_FF + c0:D_FF + c0 + ch])
        a_ref[:, c0:c0 + ch] = (g / (1.0 + jnp.exp(-g)) * u).astype(BF16)
    y2 = _dot(a_ref[...], wd_ref[...])
    out_ref[0] = x1 + gate_f * _rms(y2, gqf_ref[...])


def _mix_ffn(x, oa, la, ob, oc, od, mod, g_post_mix, g_pre_ffn, g_post_ffn, w_out, w_gu, w_down, layer):
    bsz, t, d = x.shape
    tm = TM_PROJ
    tok = lambda w: pl.BlockSpec((1, tm, w), lambda b, i: (b, i, 0))
    lsel = lambda b, i: (layer, 0, 0)
    once = dict(pipeline_mode=pl.Buffered(1))
    return pl.pallas_call(
        _mix_ffn_kernel,
        grid=(bsz, t // tm),
        in_specs=[tok(d)] + [tok(GROUP_W)] * 9 +
                 [pl.BlockSpec((None, 1, 1, 6 * d), lambda b, i: (layer, b, 0, 0)),
                  pl.BlockSpec((None, 1, d), lsel), pl.BlockSpec((None, 1, d), lsel),
                  pl.BlockSpec((None, 1, d), lsel),
                  pl.BlockSpec((None, d, d), lsel, **once),
                  pl.BlockSpec((None, d, 2 * D_FF), lsel, **once),
                  pl.BlockSpec((None, D_FF, d), lsel, **once)],
        out_specs=tok(d),
        out_shape=jax.ShapeDtypeStruct((bsz, t, d), F32),
        scratch_shapes=[pltpu.VMEM((tm, D_FF), BF16)],
        compiler_params=_cparams(("arbitrary", "arbitrary")),
        name="mix_ffn",
    )(x, *oa, *la, ob, oc, od, mod, g_post_mix, g_pre_ffn, g_post_ffn, w_out, w_gu, w_down)


def _rot_cols(w):
    half = w.shape[-1] // 2
    return jnp.concatenate([-w[..., half:], w[..., :half]], axis=-1)


def _prep_w_in(w_in):
    n_layers, d, _ = w_in.shape
    z64 = jnp.zeros((n_layers, d, 64), w_in.dtype)
    z32 = jnp.zeros((n_layers, d, 32), w_in.dtype)
    col = lambda a, b: w_in[:, :, a:b]
    kr = col(2560, 2592)
    qb = 768
    vb = 1152
    parts = [col(2048, 2432), col(2432, 2560),
             z64, kr, z32,
             z64, _rot_cols(kr), z32,
             col(0, 768),
             col(qb, qb + 64), col(qb + 128, qb + 192), col(qb + 64, qb + 128), col(qb + 192, qb + 256),
             col(vb, vb + 64), col(vb, vb + 64), col(vb + 64, vb + 128), col(vb + 64, vb + 128),
             col(1280, 2048),
             col(1024, 1152)]
    return jnp.concatenate(parts, axis=-1).astype(BF16)


def _prep_w_uq(w_uq):
    n_layers, r, _ = w_uq.shape
    z64 = jnp.zeros((n_layers, r, 64), w_uq.dtype)
    z32 = jnp.zeros((n_layers, r, 32), w_uq.dtype)
    main, rot = [], []
    for h in range(N_HEADS):
        base = h * (HEAD_DIM + D_ROPE)
        nope = w_uq[:, :, base:base + HEAD_DIM]
        rope = w_uq[:, :, base + HEAD_DIM:base + HEAD_DIM + D_ROPE]
        main += [nope, rope, z32]
        rot += [z64, _rot_cols(rope), z32]
    return jnp.concatenate(main + rot, axis=-1).astype(BF16)


def _prep_w_ukv(w_ukv):
    n_layers, r, _ = w_ukv.shape
    z64 = jnp.zeros((n_layers, r, 64), w_ukv.dtype)
    ks, vs = [], []
    for h in range(N_HEADS):
        base = h * 2 * HEAD_DIM
        ks += [w_ukv[:, :, base:base + HEAD_DIM], z64]
        vs.append(w_ukv[:, :, base + HEAD_DIM:base + 2 * HEAD_DIM])
    return jnp.concatenate(ks + vs, axis=-1).astype(BF16)


def _col_scale():
    cs = np.ones((1, RAW_PASS), np.float32)
    cs[0, Z_QA - Z_QA:Z_QA - Z_QA + 256] = HEAD_DIM ** -0.5
    cs[0, Z_QB - Z_QA:Z_QB - Z_QA + 256] = HEAD_DIM ** -0.5
    cs[0, Z_QC - Z_QA:Z_QC - Z_QA + 256] = C_QK ** -0.5
    return jnp.asarray(cs)


_A_GROUPS = tuple((0, GROUP_W, 64 * h, 64 * h + 64, 64 * h, 64 * h + 64, 8 + h, None) for h in range(4))
_B_GROUPS = tuple((128 * r, 128, 64 * g, 64 * g + 64, 128 * g + 64 * r, 128 * g + 64 * r + 64, 2 * g + r, 2 * g + r)
                  for g in range(2) for r in range(2))


def kernel(x, c, positions, w_ada, b_ada, g_pre_mix, g_post_mix, w_in, sink_logits, lam_q1, lam_k1, lam_q2, lam_k2, g_diff, g_mla_q, g_mla_kv, w_uq, w_ukv, w_out, g_pre_ffn, g_post_ffn, w_gate_up, w_down):
    bsz, t, d = x.shape
    n_layers = w_in.shape[0]
    pos = positions.astype(F32)
    pos_col = pos.reshape(bsz, t, 1)
    pos_rows = pos.reshape(bsz, t // TK_DENSE, 1, TK_DENSE)
    j = jnp.arange(1, N_ALIBI + 1, dtype=F32)
    slopes = jnp.exp2(-8.0 * j / N_ALIBI)
    half = D_ROPE // 2
    inv = jnp.power(ROPE_THETA, -jnp.arange(half, dtype=F32) / half)
    inv_full = jnp.concatenate([jnp.zeros((64,), F32), inv, inv, jnp.zeros((32,), F32)]).reshape(1, LANE)

    w1 = _prep_w_in(w_in)
    wq2 = _prep_w_uq(w_uq)
    wkv2 = _prep_w_ukv(w_ukv)
    w_out_b = w_out.astype(BF16)
    w_gu_b = w_gate_up.astype(BF16)
    w_down_b = w_down.astype(BF16)
    col_scale = _col_scale()
    r3 = lambda a: a.reshape(n_layers, 1, a.shape[-1])
    g_diff2 = r3(jnp.concatenate([g_diff, g_diff], axis=-1))
    no_sink = jnp.zeros((1,), F32)

    mod = _ada_modulation(c, w_ada, b_ada).reshape(n_layers, bsz, 1, 6 * d)
    cos_t, sin_t = _rope_tables(pos_col, inv_full)

    for layer in range(n_layers):
        z = _input_projection(x, mod, r3(g_pre_mix), w1, col_scale, r3(g_mla_q), r3(g_mla_kv),
                              wq2, wkv2, cos_t, sin_t, layer)
        oa, la = [], []
        for dil in DILATIONS:
            o_i, l_i = _banded_attention(slopes, no_sink, z, pos, dil=dil, radius=A_RADIUS, groups=_A_GROUPS,
                                         q_col=Z_QA, q_w=GROUP_W, k_col=Z_KA, k_w=GROUP_W, v_col=Z_VA,
                                         with_lse=True, out_dtype=F32, name="dilated_attention_%d" % dil)
            oa.append(o_i)
            la.append(l_i)
        (ob,) = _banded_attention(slopes, sink_logits[layer], z, pos, dil=1, radius=B_RADIUS, groups=_B_GROUPS,
                                  q_col=Z_QB, q_w=GROUP_W, k_col=Z_KB, k_w=LANE, v_col=Z_VB,
                                  with_lse=False, out_dtype=BF16, name="windowed_attention")
        oc = _diff_attention(slopes, r3(lam_q1), r3(lam_k1), r3(lam_q2), r3(lam_k2), g_diff2, z,
                             pos_col, pos_rows, layer)
        od = _mla_attention(z)
        x = _mix_ffn(x, oa, la, ob, oc, od, mod, r3(g_post_mix), r3(g_pre_ffn), r3(g_post_ffn),
                     w_out_b, w_gu_b, w_down_b, layer)
    return x
```

```python
import functools
import math

import numpy as np
import jax
import jax.numpy as jnp
from jax import lax
from jax.experimental import pallas as pl
from jax.experimental.pallas import tpu as pltpu

F32 = jnp.float32
BF16 = jnp.bfloat16

D_MODEL = 1024
HEAD_DIM = 64
N_HEADS = 4
GROUP_W = N_HEADS * HEAD_DIM
DILATIONS = (1, 4, 16)
A_RADIUS = 64
B_RADIUS = 128
C_QK = 32
D_Q_RANK = 384
D_KV_RANK = 128
D_ROPE = 32
D_FF = 2816
ROPE_THETA = 10000.0
N_ALIBI = 12
EPS = 1e-6
NEG = -1e30
LANE = 128

RAW_D = 768
RAW_PASS = 2176
RAW_W = RAW_D + RAW_PASS
Z_QD, Z_KD, Z_VD = 0, 512, 1024
Z_QA, Z_KA, Z_VA = 1280, 1536, 1792
Z_QB, Z_VB = 2048, 2304
Z_QC, Z_KC, Z_VC = 2560, 2816, 3072
Z_KB = 3328
Z_PAD = 3456
Z_W = 3584

TM_PROJ = 512
TQ_DIFF = 256
TQ_MLA = 512
TK_DENSE = 512
V_ROWS = 2 * HEAD_DIM + 16
LOG2E = 1.4426950408889634
BAND_BLK = 128
VMEM_LIMIT = 56 * 1024 * 1024


def _cparams(sem):
    return pltpu.CompilerParams(dimension_semantics=sem, vmem_limit_bytes=VMEM_LIMIT)


def _rms(x, g):
    ms = jnp.mean(x * x, axis=-1, keepdims=True)
    return x * lax.rsqrt(ms + EPS) * g


def _dot(a, b):
    return jnp.dot(a, b, preferred_element_type=F32)


def _dot_nt(a, b):
    return lax.dot_general(a, b, (((1,), (1,)), ((), ())), preferred_element_type=F32)


def _lane_iota(w):
    return lax.broadcasted_iota(jnp.int32, (1, w), 1)


def _lane_mask(w, lo, hi):
    lane = _lane_iota(w)
    return (lane >= lo) & (lane < hi)


def _ada_kernel(c_ref, w_ref, b_ref, o_ref):
    c = c_ref[...]
    c_act = c / (1.0 + jnp.exp(-c))
    o_ref[0] = jnp.dot(c_act, w_ref[0], precision=lax.Precision.HIGHEST,
                       preferred_element_type=F32) + b_ref[0]


def _ada_modulation(c, w_ada, b_ada):
    n_layers, d, six_d = w_ada.shape
    bsz = c.shape[0]
    tn = 1024
    return pl.pallas_call(
        _ada_kernel,
        grid=(n_layers, six_d // tn),
        in_specs=[pl.BlockSpec((bsz, d), lambda l, j: (0, 0)),
                  pl.BlockSpec((1, d, tn), lambda l, j: (l, 0, j)),
                  pl.BlockSpec((1, 1, tn), lambda l, j: (l, 0, j))],
        out_specs=pl.BlockSpec((1, bsz, tn), lambda l, j: (l, 0, j)),
        out_shape=jax.ShapeDtypeStruct((n_layers, bsz, six_d), F32),
        compiler_params=_cparams(("arbitrary", "arbitrary")),
        name="ada_modulation",
    )(c, w_ada, b_ada.reshape(n_layers, 1, six_d))


def _rope_kernel(p_ref, inv_ref, cos_ref, sin_ref):
    ang = p_ref[0] * inv_ref[...]
    cos_ref[0] = jnp.cos(ang)
    sin_ref[0] = jnp.sin(ang)


def _rope_tables(pos_col, inv_full):
    bsz, t, _ = pos_col.shape
    tm = 1024
    spec = pl.BlockSpec((1, tm, LANE), lambda b, i: (b, i, 0))
    return pl.pallas_call(
        _rope_kernel,
        grid=(bsz, t // tm),
        in_specs=[pl.BlockSpec((1, tm, 1), lambda b, i: (b, i, 0)),
                  pl.BlockSpec((1, LANE), lambda b, i: (0, 0))],
        out_specs=[spec, spec],
        out_shape=[jax.ShapeDtypeStruct((bsz, t, LANE), F32)] * 2,
        compiler_params=_cparams(("arbitrary", "arbitrary")),
        name="rope_tables",
    )(pos_col, inv_full)


def _inproj_kernel(x_ref, mod_ref, g_ref, w1_ref, cs_ref, gq_ref, gkv_ref, wq2_ref, wkv2_ref,
                   cos_ref, sin_ref, z_ref, *, d_scale):
    x = x_ref[0]
    mod = mod_ref[0]
    shift, scale = mod[:, 0:D_MODEL], mod[:, D_MODEL:2 * D_MODEL]
    h = (_rms(x, g_ref[...]) * (1.0 + scale) + shift).astype(BF16)
    raw = _dot(h, w1_ref[...])
    cq = raw[:, 0:D_Q_RANK]
    ckv = raw[:, D_Q_RANK:D_Q_RANK + D_KV_RANK]
    kr_a = raw[:, 512:640]
    kr_b = raw[:, 640:768]
    cos = cos_ref[0]
    sin = sin_ref[0]
    cos4 = jnp.concatenate([cos] * N_HEADS, axis=1)
    sin4 = jnp.concatenate([sin] * N_HEADS, axis=1)
    q2 = _dot(_rms(cq, gq_ref[...]).astype(BF16), wq2_ref[...])
    qd = (q2[:, 0:512] * cos4 + q2[:, 512:1024] * sin4) * d_scale
    kv2 = _dot(_rms(ckv, gkv_ref[...]).astype(BF16), wkv2_ref[...])
    kr = kr_a * cos + kr_b * sin
    kd = kv2[:, 0:512] + jnp.concatenate([kr] * N_HEADS, axis=1)
    z_ref[0, :, Z_QD:Z_QD + 512] = qd.astype(BF16)
    z_ref[0, :, Z_KD:Z_KD + 512] = kd.astype(BF16)
    z_ref[0, :, Z_VD:Z_VD + 256] = kv2[:, 512:768].astype(BF16)
    z_ref[0, :, Z_QA:Z_PAD] = (raw[:, RAW_D:RAW_W] * cs_ref[...]).astype(BF16)
    z_ref[0, :, Z_PAD:Z_W] = jnp.zeros((x.shape[0], Z_W - Z_PAD), BF16)


def _input_projection(x, mod, g_pre, w1, col_scale, g_q, g_kv, wq2, wkv2, cos_t, sin_t, layer):
    bsz, t, d = x.shape
    tm = TM_PROJ
    const = lambda b, i: (0, 0)
    lsel = lambda b, i: (layer, 0, 0)
    kern = functools.partial(_inproj_kernel, d_scale=(HEAD_DIM + D_ROPE) ** -0.5 * LOG2E)
    return pl.pallas_call(
        kern,
        grid=(bsz, t // tm),
        in_specs=[pl.BlockSpec((1, tm, d), lambda b, i: (b, i, 0)),
                  pl.BlockSpec((None, 1, 1, 6 * d), lambda b, i: (layer, b, 0, 0)),
                  pl.BlockSpec((None, 1, d), lsel),
                  pl.BlockSpec((None, d, RAW_W), lsel),
                  pl.BlockSpec((1, RAW_PASS), const),
                  pl.BlockSpec((None, 1, D_Q_RANK), lsel),
                  pl.BlockSpec((None, 1, D_KV_RANK), lsel),
                  pl.BlockSpec((None, D_Q_RANK, 1024), lsel),
                  pl.BlockSpec((None, D_KV_RANK, 768), lsel),
                  pl.BlockSpec((1, tm, LANE), lambda b, i: (b, i, 0)),
                  pl.BlockSpec((1, tm, LANE), lambda b, i: (b, i, 0))],
        out_specs=pl.BlockSpec((1, tm, Z_W), lambda b, i: (b, i, 0)),
        out_shape=jax.ShapeDtypeStruct((bsz, t, Z_W), BF16),
        compiler_params=_cparams(("arbitrary", "arbitrary")),
        name="input_projection",
    )(x, mod, g_pre, w1, col_scale, g_q, g_kv, wq2, wkv2, cos_t, sin_t)


def _banded_kernel(sl_ref, sink_ref, q_ref, k_ref, v_ref, pc_ref, pr_ref, *rest,
                   n, radius, groups, with_lse):
    if with_lse:
        o_ref, lse_ref, vm_ref = rest
    else:
        o_ref, vm_ref = rest
        lse_ref = None
    blk = BAND_BLK
    win = min(3 * blk, n)
    n_g = len(groups)
    v_all = v_ref[...]
    lane_o = _lane_iota(GROUP_W)
    for g, grp in enumerate(groups):
        vmask = (lane_o >= grp[4]) & (lane_o < grp[5])
        vm_ref[g] = jnp.where(vmask, v_all, jnp.zeros_like(v_all))

    def body(i, carry):
        r0 = pl.multiple_of(i * blk, blk)
        start = pl.multiple_of(jnp.clip(r0 - blk, 0, n - win), blk)
        sblk = start // blk
        q = q_ref[pl.ds(r0, blk), :]
        parts = []
        for (q_lo, q_w, qm_lo, qm_hi, _, _, _, _) in groups:
            qg = q[:, q_lo:q_lo + q_w]
            parts.append(jnp.where(_lane_mask(q_w, qm_lo, qm_hi), qg, jnp.zeros_like(qg)))
        qs = jnp.concatenate(parts, axis=0)
        kw = k_ref[pl.ds(start, win), :]
        s = _dot_nt(qs, kw)
        pq = pc_ref[pl.ds(r0, blk), :]
        pk = jnp.concatenate([pr_ref[sblk + j] for j in range(win // blk)], axis=1)
        dist = jnp.abs(pq - pk)
        qi = r0 + lax.broadcasted_iota(jnp.int32, (blk, 1), 0)
        ki = start + lax.broadcasted_iota(jnp.int32, (1, win), 1)
        valid = jnp.abs(qi - ki) <= radius
        p_parts, inv_parts, lse_parts = [], [], []
        for g, grp in enumerate(groups):
            sg = s[g * blk:(g + 1) * blk] - sl_ref[grp[6]] * dist
            sg = jnp.where(valid, sg, NEG)
            m = jnp.max(sg, axis=1, keepdims=True)
            if grp[7] is not None:
                sk = sink_ref[grp[7]]
                m = jnp.maximum(m, sk)
            p = jnp.exp(sg - m)
            den = jnp.sum(p, axis=1, keepdims=True)
            if grp[7] is not None:
                den = den + jnp.exp(sk - m)
            p_parts.append(p.astype(BF16))
            inv_parts.append(1.0 / den)
            lse_parts.append(m + jnp.log(den))
        pcat = jnp.concatenate(p_parts, axis=1)
        vcat = jnp.concatenate([vm_ref[g, pl.ds(start, win), :] for g in range(n_g)], axis=0)
        o = _dot(pcat, vcat)
        inv_full = jnp.zeros((blk, GROUP_W), F32)
        lse_full = jnp.zeros((blk, GROUP_W), F32)
        for g, grp in enumerate(groups):
            vmask = (lane_o >= grp[4]) & (lane_o < grp[5])
            inv_full = jnp.where(vmask, inv_parts[g], inv_full)
            lse_full = jnp.where(vmask, lse_parts[g], lse_full)
        o_ref[pl.ds(r0, blk), :] = (o * inv_full).astype(o_ref.dtype)
        if with_lse:
            lse_ref[pl.ds(r0, blk), :] = lse_full
        return carry

    lax.fori_loop(0, n // blk, body, 0)


def _banded_attention(slopes, sink, z, pos, *, dil, radius, groups, q_col, q_w, k_col, k_w, v_col,
                      with_lse, out_dtype, name):
    bsz, t, zw = z.shape
    n = t // dil
    blk = BAND_BLK
    zf = z.reshape(bsz, n, dil * zw)
    pos_f = pos.reshape(bsz, n, dil).transpose(0, 2, 1)
    pos_col = pos_f.reshape(bsz, dil, n, 1)
    pos_row = pos_f.reshape(bsz, dil, n // blk, 1, blk)
    kern = functools.partial(_banded_kernel, n=n, radius=radius, groups=groups, with_lse=with_lse)
    o_spec = pl.BlockSpec((None, n, GROUP_W), lambda b, r: (b, 0, r))
    out_specs = [o_spec]
    out_shape = [jax.ShapeDtypeStruct((bsz, n, dil * GROUP_W), out_dtype)]
    if with_lse:
        out_specs.append(o_spec)
        out_shape.append(jax.ShapeDtypeStruct((bsz, n, dil * GROUP_W), F32))
    smem = pl.BlockSpec(memory_space=pltpu.SMEM)

    def zspec(col, w):
        return pl.BlockSpec((None, n, w), lambda b, r: (b, 0, r * (zw // w) + col // w))

    outs = pl.pallas_call(
        kern,
        grid=(bsz, dil),
        in_specs=[smem, smem, zspec(q_col, q_w), zspec(k_col, k_w), zspec(v_col, GROUP_W),
                  pl.BlockSpec((None, None, n, 1), lambda b, r: (b, r, 0, 0)),
                  pl.BlockSpec((None, None, n // blk, 1, blk), lambda b, r: (b, r, 0, 0, 0))],
        out_specs=out_specs,
        out_shape=out_shape,
        scratch_shapes=[pltpu.VMEM((len(groups), n, GROUP_W), BF16)],
        compiler_params=_cparams(("arbitrary", "arbitrary")),
        name=name,
    )(slopes, sink, zf, zf, zf, pos_col, pos_row)
    return [o.reshape(bsz, t, GROUP_W) for o in outs]


def _cat_lanes(xs):
    return xs[0] if len(xs) == 1 else jnp.concatenate(xs, axis=1)


def _flash_loop(n_kb, scores_fn, vs_fn, n_maps, tq, tk, s_refs, p_refs, acc_ref):
    n = n_maps * tq
    n_g = 2 * n_maps
    n_l = V_ROWS - 2 * HEAD_DIM
    hd = HEAD_DIM

    def stage_scores(kb, s_ref):
        col_max = []
        for g, s in enumerate(scores_fn(kb)):
            s_ref[:, g * tq:(g + 1) * tq] = s
            col_max.append(jnp.max(s, axis=0, keepdims=True))
        return col_max

    def stage_softmax(s_ref, p_ref, m, col_max):
        m_new, alphas = [], []
        for g in range(n_g):
            mn = jnp.maximum(m[g], col_max[g])
            alphas.append(jnp.exp2(m[g] - mn))
            m_new.append(mn)
            h, mp = divmod(g, n_maps)
            p_ref[h * tk:(h + 1) * tk, mp * tq:(mp + 1) * tq] = (
                jnp.exp2(s_ref[:, g * tq:(g + 1) * tq] - mn).astype(BF16))
        a0 = _cat_lanes(alphas[0:n_maps])
        a1 = _cat_lanes(alphas[n_maps:n_g])
        row = lax.broadcasted_iota(jnp.int32, (n_l, 1), 0)
        return m_new, (a0, a1, jnp.where(row == 0, a0, a1))

    def stage_pv(kb, p_ref, alphas):
        pv = _dot(vs_fn(kb), p_ref[...])
        acc_ref[0:hd] = alphas[0] * acc_ref[0:hd] + pv[0:hd]
        acc_ref[hd:2 * hd] = alphas[1] * acc_ref[hd:2 * hd] + pv[hd:2 * hd]
        acc_ref[2 * hd:V_ROWS] = alphas[2] * acc_ref[2 * hd:V_ROWS] + pv[2 * hd:V_ROWS]

    def step(kb, cur, m, col_max, a_prev):
        nxt = 1 - cur
        col_max_next = stage_scores(jnp.minimum(kb + 1, n_kb - 1), s_refs[nxt])
        stage_pv(jnp.maximum(kb - 1, 0), p_refs[nxt], a_prev)
        m, a_cur = stage_softmax(s_refs[cur], p_refs[cur], m, col_max)
        return m, col_max_next, a_cur

    def body(i, carry):
        m, col_max, a_prev = carry
        m, col_max, a_prev = step(2 * i, 0, m, col_max, a_prev)
        return step(2 * i + 1, 1, m, col_max, a_prev)

    acc_ref[...] = jnp.zeros((V_ROWS, n), F32)
    p_refs[1][...] = jnp.zeros((2 * tk, n), BF16)
    init = ([jnp.full((1, tq), NEG, F32)] * n_g, stage_scores(0, s_refs[0]),
            (jnp.ones((1, n), F32), jnp.ones((1, n), F32), jnp.ones((n_l, n), F32)))
    _, _, a_last = lax.fori_loop(0, n_kb // 2, body, init)
    stage_pv(n_kb - 1, p_refs[1], a_last)
    return acc_ref[0:hd], acc_ref[hd:2 * hd], acc_ref[2 * hd:V_ROWS]


def _flash_scratch(n_maps, tq, tk):
    n = n_maps * tq
    return ([pltpu.VMEM((tk, 2 * n), F32)] * 2 + [pltpu.VMEM((2 * tk, n), BF16)] * 2
            + [pltpu.VMEM((V_ROWS, n), F32)])


def _fill_vt(v_ref, vt_ref, t, tk):
    v_t = v_ref[0].astype(F32).T
    row = lax.broadcasted_iota(jnp.int32, (V_ROWS - 2 * HEAD_DIM, tk), 0)
    zeros = jnp.zeros((HEAD_DIM, tk), F32)
    for kb in range(t // tk):
        blk = v_t[:, kb * tk:(kb + 1) * tk]
        for h in range(2):
            top = blk[0:HEAD_DIM] if h == 0 else zeros
            bot = zeros if h == 0 else blk[HEAD_DIM:2 * HEAD_DIM]
            ones = jnp.where(row == h, 1.0, 0.0)
            vt_ref[h, kb] = jnp.concatenate([top, bot, ones], axis=0).astype(BF16)


def _diff_kernel(sl_ref, lq1_ref, lk1_ref, lq2_ref, lk2_ref, gd_ref, q_ref, k_ref, v_ref, pq_ref,
                 pk_ref, o_ref, vt_ref, s0_ref, s1_ref, p0_ref, p1_ref, acc_ref, *, t, lam_init):
    tq, tk = TQ_DIFF, TK_DENSE
    hp = pl.program_id(1)

    @pl.when(pl.program_id(2) == 0)
    def _():
        _fill_vt(v_ref, vt_ref, t, tk)

    q_t = q_ref[0].astype(F32).T
    row = lax.broadcasted_iota(jnp.int32, (LANE, 1), 0)
    qs_t = jnp.concatenate(
        [jnp.where((row >= C_QK * g) & (row < C_QK * (g + 1)), q_t, 0.0) for g in range(4)],
        axis=1).astype(BF16)
    pq = pq_ref[0, 0]
    slope0 = sl_ref[N_HEADS + 2 * hp] * LOG2E
    slope1 = sl_ref[N_HEADS + 2 * hp + 1] * LOG2E

    def scores(kb):
        off = pl.multiple_of(kb * tk, tk)
        kk = k_ref[0, pl.ds(off, tk), :]
        pk = pk_ref[0, pl.ds(off, tk), :]
        dist = jnp.abs(jnp.concatenate([pk] * (tq // LANE), axis=1) - pq)
        bias = [slope0 * dist, slope1 * dist]
        return [_dot(kk, qs_t[:, g * tq:(g + 1) * tq]) - bias[g // 2] for g in range(4)]

    def values(kb):
        return jnp.concatenate([vt_ref[0, kb], vt_ref[1, kb]], axis=1)

    acc0, acc1, acc_l = _flash_loop(t // tk, scores, values, 2, tq, tk,
                                    (s0_ref, s1_ref), (p0_ref, p1_ref), acc_ref)
    lam =(jnp.exp(jnp.sum(lq1_ref[...] * lk1_ref[...], axis=1, keepdims=True))
           - jnp.exp(jnp.sum(lq2_ref[...] * lk2_ref[...], axis=1, keepdims=True)) + lam_init)
    o0 = acc0 / acc_l[0:1]
    o1 = acc1 / acc_l[1:2]
    o_t = jnp.concatenate([o0[:, 0:tq] - lam * o0[:, tq:2 * tq],
                           o1[:, 0:tq] - lam * o1[:, tq:2 * tq]], axis=0)
    o = o_t.T
    lane = _lane_iota(LANE)
    o_sq = o * o
    ms0 = jnp.sum(jnp.where(lane < HEAD_DIM, o_sq, 0.0), axis=1, keepdims=True) * (1.0 / HEAD_DIM)
    ms1 = jnp.sum(jnp.where(lane >= HEAD_DIM, o_sq, 0.0), axis=1, keepdims=True) * (1.0 / HEAD_DIM)
    ms = jnp.where(lane < HEAD_DIM, ms0, ms1)
    y = o * lax.rsqrt(ms + EPS) * gd_ref[...]
    o_ref[0] = (y * (1.0 - lam_init)).astype(o_ref.dtype)


def _diff_attention(slopes, lq1, lk1, lq2, lk2, g_diff2, z, pos_q_rows, pos_k_lanes, layer):
    bsz, t, _ = z.shape
    tq, tk = TQ_DIFF, TK_DENSE
    lam_init = 0.8 - 0.6 * math.exp(-0.3 * layer)
    kern = functools.partial(_diff_kernel, t=t, lam_init=lam_init)
    lsel = lambda b, h, i: (layer, 0, 0)
    return pl.pallas_call(
        kern,
        grid=(bsz, 2, t // tq),
        in_specs=[pl.BlockSpec(memory_space=pltpu.SMEM),
                  pl.BlockSpec((None, 1, C_QK), lsel), pl.BlockSpec((None, 1, C_QK), lsel),
                  pl.BlockSpec((None, 1, C_QK), lsel), pl.BlockSpec((None, 1, C_QK), lsel),
                  pl.BlockSpec((None, 1, LANE), lsel),
                  pl.BlockSpec((1, tq, LANE), lambda b, h, i: (b, i, Z_QC // LANE + h)),
                  pl.BlockSpec((1, t, LANE), lambda b, h, i: (b, 0, Z_KC // LANE + h)),
                  pl.BlockSpec((1, t, LANE), lambda b, h, i: (b, 0, Z_VC // LANE + h)),
                  pl.BlockSpec((1, 1, 1, tq), lambda b, h, i: (b, i, 0, 0)),
                  pl.BlockSpec((1, t, LANE), lambda b, h, i: (b, 0, 0))],
        out_specs=pl.BlockSpec((1, tq, LANE), lambda b, h, i: (b, i, h)),
        out_shape=jax.ShapeDtypeStruct((bsz, t, GROUP_W), BF16),
        scratch_shapes=[pltpu.VMEM((2, t // tk, V_ROWS, tk), BF16)] + _flash_scratch(2, tq, tk),
        compiler_params=_cparams(("arbitrary", "arbitrary", "arbitrary")),
        name="diff_attention",
    )(slopes, lq1, lk1, lq2, lk2, g_diff2, z, z, z, pos_q_rows, pos_k_lanes)


def _mla_kernel(q_ref, k_ref, v_ref, o_ref, vt_ref, s0_ref, s1_ref, p0_ref, p1_ref, acc_ref, *, t):
    tq, tk = TQ_MLA, TK_DENSE

    @pl.when(pl.program_id(2) == 0)
    def _():
        _fill_vt(v_ref, vt_ref, t, tk)

    q = q_ref[0].astype(F32)
    q_t = [q[:, h * LANE:(h + 1) * LANE].T.astype(BF16) for h in range(2)]

    def scores(kb):
        off = pl.multiple_of(kb * tk, tk)
        kk = k_ref[0, pl.ds(off, tk), :]
        return [_dot(kk[:, h * LANE:(h + 1) * LANE], q_t[h]) for h in range(2)]

    def values(kb):
        return jnp.concatenate([vt_ref[0, kb], vt_ref[1, kb]], axis=1)

    acc0, acc1, acc_l = _flash_loop(t // tk, scores, values, 1, tq, tk,
                                    (s0_ref, s1_ref), (p0_ref, p1_ref), acc_ref)
    o_t = jnp.concatenate([acc0 / acc_l[0:1], acc1 / acc_l[1:2]], axis=0)
    o_ref[0] = o_t.T.astype(o_ref.dtype)


def _mla_attention(z):
    bsz, t, _ = z.shape
    tq, tk = TQ_MLA, TK_DENSE
    kern = functools.partial(_mla_kernel, t=t)
    return pl.pallas_call(
        kern,
        grid=(bsz, 2, t // tq),
        in_specs=[pl.BlockSpec((1, tq, 2 * LANE), lambda b, h, i: (b, i, Z_QD // 256 + h)),
                  pl.BlockSpec((1, t, 2 * LANE), lambda b, h, i: (b, 0, Z_KD // 256 + h)),
                  pl.BlockSpec((1, t, LANE), lambda b, h, i: (b, 0, Z_VD // LANE + h))],
        out_specs=pl.BlockSpec((1, tq, LANE), lambda b, h, i: (b, i, h)),
        out_shape=jax.ShapeDtypeStruct((bsz, t, GROUP_W), BF16),
        scratch_shapes=[pltpu.VMEM((2, t // tk, V_ROWS, tk), BF16)] + _flash_scratch(1, tq, tk),
        compiler_params=_cparams(("arbitrary", "arbitrary", "arbitrary")),
        name="mla_attention",
    )(z, z, z)


def _mix_ffn_kernel(x_ref, oa1_ref, oa2_ref, oa3_ref, la1_ref, la2_ref, la3_ref, ob_ref, oc_ref, od_ref,
                    mod_ref, gpm_ref, gpf_ref, gqf_ref, wout_ref, wgu_ref, wd_ref, out_ref, a_ref):
    d = D_MODEL
    x = x_ref[0]
    mod = mod_ref[0]
    gate_m = mod[:, 2 * d:3 * d]
    shift_f, scale_f, gate_f = mod[:, 3 * d:4 * d], mod[:, 4 * d:5 * d], mod[:, 5 * d:6 * d]
    l1, l2, l3 = la1_ref[0], la2_ref[0], la3_ref[0]
    mx = jnp.maximum(jnp.maximum(l1, l2), l3)
    e1, e2, e3 = jnp.exp(l1 - mx), jnp.exp(l2 - mx), jnp.exp(l3 - mx)
    mix_a = (e1 * oa1_ref[0] + e2 * oa2_ref[0] + e3 * oa3_ref[0]) / (e1 + e2 + e3)
    mix = jnp.concatenate([mix_a.astype(BF16), ob_ref[0], oc_ref[0], od_ref[0]], axis=1)
    y = _dot(mix, wout_ref[...])
    x1 = x + gate_m * _rms(y, gpm_ref[...])
    h = (_rms(x1, gpf_ref[...]) * (1.0 + scale_f) + shift_f).astype(BF16)
    ch = 256
    for c0 in range(0, D_FF, ch):
        g = _dot(h, wgu_ref[:, c0:c0 + ch])
        u = _dot(h, wgu_ref[:, D_FF + c0:D_FF + c0 + ch])
        a_ref[:, c0:c0 + ch] = (g / (1.0 + jnp.exp(-g)) * u).astype(BF16)
    y2 = _dot(a_ref[...], wd_ref[...])
    out_ref[0] = x1 + gate_f * _rms(y2, gqf_ref[...])


def _mix_ffn(x, oa, la, ob, oc, od, mod, g_post_mix, g_pre_ffn, g_post_ffn, w_out, w_gu, w_down, layer):
    bsz, t, d = x.shape
    tm = TM_PROJ
    tok = lambda w: pl.BlockSpec((1, tm, w), lambda b, i: (b, i, 0))
    lsel = lambda b, i: (layer, 0, 0)
    once = dict(pipeline_mode=pl.Buffered(1))
    return pl.pallas_call(
        _mix_ffn_kernel,
        grid=(bsz, t // tm),
        in_specs=[tok(d)] + [tok(GROUP_W)] * 9 +
                 [pl.BlockSpec((None, 1, 1, 6 * d), lambda b, i: (layer, b, 0, 0)),
                  pl.BlockSpec((None, 1, d), lsel), pl.BlockSpec((None, 1, d), lsel),
                  pl.BlockSpec((None, 1, d), lsel),
                  pl.BlockSpec((None, d, d), lsel, **once),
                  pl.BlockSpec((None, d, 2 * D_FF), lsel, **once),
                  pl.BlockSpec((None, D_FF, d), lsel, **once)],
        out_specs=tok(d),
        out_shape=jax.ShapeDtypeStruct((bsz, t, d), F32),
        scratch_shapes=[pltpu.VMEM((tm, D_FF), BF16)],
        compiler_params=_cparams(("arbitrary", "arbitrary")),
        name="mix_ffn",
    )(x, *oa, *la, ob, oc, od, mod, g_post_mix, g_pre_ffn, g_post_ffn, w_out, w_gu, w_down)


def _rot_cols(w):
    half = w.shape[-1] // 2
    return jnp.concatenate([-w[..., half:], w[..., :half]], axis=-1)


def _prep_w_in(w_in):
    n_layers, d, _ = w_in.shape
    z64 = jnp.zeros((n_layers, d, 64), w_in.dtype)
    z32 = jnp.zeros((n_layers, d, 32), w_in.dtype)
    col = lambda a, b: w_in[:, :, a:b]
    kr = col(2560, 2592)
    qb = 768
    vb = 1152
    parts = [col(2048, 2432), col(2432, 2560),
             z64, kr, z32,
             z64, _rot_cols(kr), z32,
             col(0, 768),
             col(qb, qb + 64), col(qb + 128, qb + 192), col(qb + 64, qb + 128), col(qb + 192, qb + 256),
             col(vb, vb + 64), col(vb, vb + 64), col(vb + 64, vb + 128), col(vb + 64, vb + 128),
             col(1280, 2048),
             col(1024, 1152)]
    return jnp.concatenate(parts, axis=-1).astype(BF16)


def _prep_w_uq(w_uq):
    n_layers, r, _ = w_uq.shape
    z64 = jnp.zeros((n_layers, r, 64), w_uq.dtype)
    z32 = jnp.zeros((n_layers, r, 32), w_uq.dtype)
    main, rot = [], []
    for h in range(N_HEADS):
        base = h * (HEAD_DIM + D_ROPE)
        nope = w_uq[:, :, base:base + HEAD_DIM]
        rope = w_uq[:, :, base + HEAD_DIM:base + HEAD_DIM + D_ROPE]
        main += [nope, rope, z32]
        rot += [z64, _rot_cols(rope), z32]
    return jnp.concatenate(main + rot, axis=-1).astype(BF16)


def _prep_w_ukv(w_ukv):
    n_layers, r, _ = w_ukv.shape
    z64 = jnp.zeros((n_layers, r, 64), w_ukv.dtype)
    ks, vs = [], []
    for h in range(N_HEADS):
        base = h * 2 * HEAD_DIM
        ks += [w_ukv[:, :, base:base + HEAD_DIM], z64]
        vs.append(w_ukv[:, :, base + HEAD_DIM:base + 2 * HEAD_DIM])
    return jnp.concatenate(ks + vs, axis=-1).astype(BF16)


def _col_scale():
    cs = np.ones((1, RAW_PASS), np.float32)
    cs[0, Z_QA - Z_QA:Z_QA - Z_QA + 256] = HEAD_DIM ** -0.5
    cs[0, Z_QB - Z_QA:Z_QB - Z_QA + 256] = HEAD_DIM ** -0.5
    cs[0, Z_QC - Z_QA:Z_QC - Z_QA + 256] = C_QK ** -0.5 * LOG2E
    return jnp.asarray(cs)


_A_GROUPS = tuple((0, GROUP_W, 64 * h, 64 * h + 64, 64 * h, 64 * h + 64, 8 + h, None) for h in range(4))
_B_GROUPS = tuple((128 * r, 128, 64 * g, 64 * g + 64, 128 * g + 64 * r, 128 * g + 64 * r + 64, 2 * g + r, 2 * g + r)
                  for g in range(2) for r in range(2))


def kernel(x, c, positions, w_ada, b_ada, g_pre_mix, g_post_mix, w_in, sink_logits, lam_q1, lam_k1, lam_q2, lam_k2, g_diff, g_mla_q, g_mla_kv, w_uq, w_ukv, w_out, g_pre_ffn, g_post_ffn, w_gate_up, w_down):
    bsz, t, d = x.shape
    n_layers = w_in.shape[0]
    pos = positions.astype(F32)
    pos_col = pos.reshape(bsz, t, 1)
    pos_q_rows = pos.reshape(bsz, t // TQ_DIFF, 1, TQ_DIFF)
    pos_k_lanes = jnp.broadcast_to(pos[:, :, None], (bsz, t, LANE))
    j = jnp.arange(1, N_ALIBI + 1, dtype=F32)
    slopes = jnp.exp2(-8.0 * j / N_ALIBI)
    half = D_ROPE // 2
    inv = jnp.power(ROPE_THETA, -jnp.arange(half, dtype=F32) / half)
    inv_full = jnp.concatenate([jnp.zeros((64,), F32), inv, inv, jnp.zeros((32,), F32)]).reshape(1, LANE)

    w1 = _prep_w_in(w_in)
    wq2 = _prep_w_uq(w_uq)
    wkv2 = _prep_w_ukv(w_ukv)
    w_out_b = w_out.astype(BF16)
    w_gu_b = w_gate_up.astype(BF16)
    w_down_b = w_down.astype(BF16)
    col_scale = _col_scale()
    r3 = lambda a: a.reshape(n_layers, 1, a.shape[-1])
    g_diff2 = r3(jnp.concatenate([g_diff, g_diff], axis=-1))
    no_sink = jnp.zeros((1,), F32)

    mod = _ada_modulation(c, w_ada, b_ada).reshape(n_layers, bsz, 1, 6 * d)
    cos_t, sin_t = _rope_tables(pos_col, inv_full)

    for layer in range(n_layers):
        z = _input_projection(x, mod, r3(g_pre_mix), w1, col_scale, r3(g_mla_q), r3(g_mla_kv),
                              wq2, wkv2, cos_t, sin_t, layer)
        oa, la = [], []
        z_a = z[:, :, Z_QA:Z_QA + 3 * GROUP_W]
        for dil in DILATIONS:
            src, col0 = (z, Z_QA) if dil == 1 else (z_a, 0)
            o_i, l_i = _banded_attention(slopes, no_sink, src, pos, dil=dil, radius=A_RADIUS, groups=_A_GROUPS,
                                         q_col=col0, q_w=GROUP_W, k_col=col0 + GROUP_W, k_w=GROUP_W,
                                         v_col=col0 + 2 * GROUP_W,
                                         with_lse=True, out_dtype=F32, name="dilated_attention_%d" % dil)
            oa.append(o_i)
            la.append(l_i)
        (ob,) = _banded_attention(slopes, sink_logits[layer], z, pos, dil=1, radius=B_RADIUS, groups=_B_GROUPS,
                                  q_col=Z_QB, q_w=GROUP_W, k_col=Z_KB, k_w=LANE, v_col=Z_VB,
                                  with_lse=False, out_dtype=BF16, name="windowed_attention")
        oc = _diff_attention(slopes, r3(lam_q1), r3(lam_k1), r3(lam_q2), r3(lam_k2), g_diff2, z,
                             pos_q_rows, pos_k_lanes, layer)
        od = _mla_attention(z)
        x = _mix_ffn(x, oa, la, ob, oc, od, mod, r3(g_post_mix), r3(g_pre_ffn), r3(g_post_ffn),
                     w_out_b, w_gu_b, w_down_b, layer)
    return x
```

```python
import functools
import math

import numpy as np
import jax
import jax.numpy as jnp
from jax import lax
from jax.experimental import pallas as pl
from jax.experimental.pallas import tpu as pltpu

F32 = jnp.float32
BF16 = jnp.bfloat16

D_MODEL = 1024
HEAD_DIM = 64
N_HEADS = 4
GROUP_W = N_HEADS * HEAD_DIM
DILATIONS = (1, 4, 16)
A_RADIUS = 64
B_RADIUS = 128
C_QK = 32
D_Q_RANK = 384
D_KV_RANK = 128
D_ROPE = 32
D_FF = 2816
ROPE_THETA = 10000.0
N_ALIBI = 12
EPS = 1e-6
NEG = -1e30
LANE = 128

RAW_D = 768
RAW_PASS = 2176
RAW_W = RAW_D + RAW_PASS
Z_QD, Z_KD, Z_VD = 0, 512, 1024
Z_QA, Z_KA, Z_VA = 1280, 1536, 1792
Z_QB, Z_VB = 2048, 2304
Z_QC, Z_KC, Z_VC = 2560, 2816, 3072
Z_KB = 3328
Z_PAD = 3456
Z_W = 3584

TM_PROJ = 512
TQ_DIFF = 256
TQ_MLA = 512
TK_DENSE = 512
V_ROWS = 2 * HEAD_DIM + 16
LOG2E = 1.4426950408889634
FOLD_LO = 64.0
FOLD_RANGE = 16384
AUG_ROWS = 16
BAND_BLK = 128
VMEM_LIMIT = 56 * 1024 * 1024


def _cparams(sem):
    return pltpu.CompilerParams(dimension_semantics=sem, vmem_limit_bytes=VMEM_LIMIT)


def _rms(x, g):
    ms = jnp.mean(x * x, axis=-1, keepdims=True)
    return x * lax.rsqrt(ms + EPS) * g


def _dot(a, b):
    return jnp.dot(a, b, preferred_element_type=F32)


def _dot_nt(a, b):
    return lax.dot_general(a, b, (((1,), (1,)), ((), ())), preferred_element_type=F32)


def _lane_iota(w):
    return lax.broadcasted_iota(jnp.int32, (1, w), 1)


def _lane_mask(w, lo, hi):
    lane = _lane_iota(w)
    return (lane >= lo) & (lane < hi)


def _ada_kernel(c_ref, w_ref, b_ref, o_ref):
    c = c_ref[...]
    c_act = c / (1.0 + jnp.exp(-c))
    o_ref[0] = jnp.dot(c_act, w_ref[0], precision=lax.Precision.HIGHEST,
                       preferred_element_type=F32) + b_ref[0]


def _ada_modulation(c, w_ada, b_ada):
    n_layers, d, six_d = w_ada.shape
    bsz = c.shape[0]
    tn = 1024
    return pl.pallas_call(
        _ada_kernel,
        grid=(n_layers, six_d // tn),
        in_specs=[pl.BlockSpec((bsz, d), lambda l, j: (0, 0)),
                  pl.BlockSpec((1, d, tn), lambda l, j: (l, 0, j)),
                  pl.BlockSpec((1, 1, tn), lambda l, j: (l, 0, j))],
        out_specs=pl.BlockSpec((1, bsz, tn), lambda l, j: (l, 0, j)),
        out_shape=jax.ShapeDtypeStruct((n_layers, bsz, six_d), F32),
        compiler_params=_cparams(("arbitrary", "arbitrary")),
        name="ada_modulation",
    )(c, w_ada, b_ada.reshape(n_layers, 1, six_d))


def _rope_kernel(p_ref, inv_ref, cos_ref, sin_ref):
    ang = p_ref[0] * inv_ref[...]
    cos_ref[0] = jnp.cos(ang)
    sin_ref[0] = jnp.sin(ang)


def _rope_tables(pos_col, inv_full):
    bsz, t, _ = pos_col.shape
    tm = 1024
    spec = pl.BlockSpec((1, tm, LANE), lambda b, i: (b, i, 0))
    return pl.pallas_call(
        _rope_kernel,
        grid=(bsz, t // tm),
        in_specs=[pl.BlockSpec((1, tm, 1), lambda b, i: (b, i, 0)),
                  pl.BlockSpec((1, LANE), lambda b, i: (0, 0))],
        out_specs=[spec, spec],
        out_shape=[jax.ShapeDtypeStruct((bsz, t, LANE), F32)] * 2,
        compiler_params=_cparams(("arbitrary", "arbitrary")),
        name="rope_tables",
    )(pos_col, inv_full)


def _inproj_kernel(x_ref, mod_ref, g_ref, w1_ref, cs_ref, gq_ref, gkv_ref, wq2_ref, wkv2_ref,
                   cos_ref, sin_ref, z_ref, *, d_scale):
    x = x_ref[0]
    mod = mod_ref[0]
    shift, scale = mod[:, 0:D_MODEL], mod[:, D_MODEL:2 * D_MODEL]
    h = (_rms(x, g_ref[...]) * (1.0 + scale) + shift).astype(BF16)
    raw = _dot(h, w1_ref[...])
    cq = raw[:, 0:D_Q_RANK]
    ckv = raw[:, D_Q_RANK:D_Q_RANK + D_KV_RANK]
    kr_a = raw[:, 512:640]
    kr_b = raw[:, 640:768]
    cos = cos_ref[0]
    sin = sin_ref[0]
    cos4 = jnp.concatenate([cos] * N_HEADS, axis=1)
    sin4 = jnp.concatenate([sin] * N_HEADS, axis=1)
    q2 = _dot(_rms(cq, gq_ref[...]).astype(BF16), wq2_ref[...])
    qd = (q2[:, 0:512] * cos4 + q2[:, 512:1024] * sin4) * d_scale
    kv2 = _dot(_rms(ckv, gkv_ref[...]).astype(BF16), wkv2_ref[...])
    kr = kr_a * cos + kr_b * sin
    kd = kv2[:, 0:512] + jnp.concatenate([kr] * N_HEADS, axis=1)
    z_ref[0, :, Z_QD:Z_QD + 512] = qd.astype(BF16)
    z_ref[0, :, Z_KD:Z_KD + 512] = kd.astype(BF16)
    z_ref[0, :, Z_VD:Z_VD + 256] = kv2[:, 512:768].astype(BF16)
    z_ref[0, :, Z_QA:Z_PAD] = (raw[:, RAW_D:RAW_W] * cs_ref[...]).astype(BF16)
    z_ref[0, :, Z_PAD:Z_W] = jnp.zeros((x.shape[0], Z_W - Z_PAD), BF16)


def _input_projection(x, mod, g_pre, w1, col_scale, g_q, g_kv, wq2, wkv2, cos_t, sin_t, layer):
    bsz, t, d = x.shape
    tm = TM_PROJ
    const = lambda b, i: (0, 0)
    lsel = lambda b, i: (layer, 0, 0)
    kern = functools.partial(_inproj_kernel, d_scale=(HEAD_DIM + D_ROPE) ** -0.5 * LOG2E)
    return pl.pallas_call(
        kern,
        grid=(bsz, t // tm),
        in_specs=[pl.BlockSpec((1, tm, d), lambda b, i: (b, i, 0)),
                  pl.BlockSpec((None, 1, 1, 6 * d), lambda b, i: (layer, b, 0, 0)),
                  pl.BlockSpec((None, 1, d), lsel),
                  pl.BlockSpec((None, d, RAW_W), lsel),
                  pl.BlockSpec((1, RAW_PASS), const),
                  pl.BlockSpec((None, 1, D_Q_RANK), lsel),
                  pl.BlockSpec((None, 1, D_KV_RANK), lsel),
                  pl.BlockSpec((None, D_Q_RANK, 1024), lsel),
                  pl.BlockSpec((None, D_KV_RANK, 768), lsel),
                  pl.BlockSpec((1, tm, LANE), lambda b, i: (b, i, 0)),
                  pl.BlockSpec((1, tm, LANE), lambda b, i: (b, i, 0))],
        out_specs=pl.BlockSpec((1, tm, Z_W), lambda b, i: (b, i, 0)),
        out_shape=jax.ShapeDtypeStruct((bsz, t, Z_W), BF16),
        compiler_params=_cparams(("arbitrary", "arbitrary")),
        name="input_projection",
    )(x, mod, g_pre, w1, col_scale, g_q, g_kv, wq2, wkv2, cos_t, sin_t)


def _banded_kernel(sl_ref, sink_ref, q_ref, k_ref, v_ref, pc_ref, pr_ref, *rest,
                   n, radius, groups, with_lse):
    if with_lse:
        o_ref, lse_ref, vm_ref = rest
    else:
        o_ref, vm_ref = rest
        lse_ref = None
    blk = BAND_BLK
    win = min(3 * blk, n)
    n_g = len(groups)
    v_all = v_ref[...]
    lane_o = _lane_iota(GROUP_W)
    for g, grp in enumerate(groups):
        vmask = (lane_o >= grp[4]) & (lane_o < grp[5])
        vm_ref[g] = jnp.where(vmask, v_all, jnp.zeros_like(v_all))

    def body(i, carry):
        r0 = pl.multiple_of(i * blk, blk)
        start = pl.multiple_of(jnp.clip(r0 - blk, 0, n - win), blk)
        sblk = start // blk
        q = q_ref[pl.ds(r0, blk), :]
        parts = []
        for (q_lo, q_w, qm_lo, qm_hi, _, _, _, _) in groups:
            qg = q[:, q_lo:q_lo + q_w]
            parts.append(jnp.where(_lane_mask(q_w, qm_lo, qm_hi), qg, jnp.zeros_like(qg)))
        qs = jnp.concatenate(parts, axis=0)
        kw = k_ref[pl.ds(start, win), :]
        s = _dot_nt(qs, kw)
        pq = pc_ref[pl.ds(r0, blk), :]
        pk = jnp.concatenate([pr_ref[sblk + j] for j in range(win // blk)], axis=1)
        dist = jnp.abs(pq - pk)
        qi = r0 + lax.broadcasted_iota(jnp.int32, (blk, 1), 0)
        ki = start + lax.broadcasted_iota(jnp.int32, (1, win), 1)
        valid = jnp.abs(qi - ki) <= radius
        p_parts, inv_parts, lse_parts = [], [], []
        for g, grp in enumerate(groups):
            sg = s[g * blk:(g + 1) * blk] - sl_ref[grp[6]] * dist
            sg = jnp.where(valid, sg, NEG)
            m = jnp.max(sg, axis=1, keepdims=True)
            if grp[7] is not None:
                sk = sink_ref[grp[7]]
                m = jnp.maximum(m, sk)
            p = jnp.exp(sg - m)
            den = jnp.sum(p, axis=1, keepdims=True)
            if grp[7] is not None:
                den = den + jnp.exp(sk - m)
            p_parts.append(p.astype(BF16))
            inv_parts.append(1.0 / den)
            lse_parts.append(m + jnp.log(den))
        pcat = jnp.concatenate(p_parts, axis=1)
        vcat = jnp.concatenate([vm_ref[g, pl.ds(start, win), :] for g in range(n_g)], axis=0)
        o = _dot(pcat, vcat)
        inv_full = jnp.zeros((blk, GROUP_W), F32)
        lse_full = jnp.zeros((blk, GROUP_W), F32)
        for g, grp in enumerate(groups):
            vmask = (lane_o >= grp[4]) & (lane_o < grp[5])
            inv_full = jnp.where(vmask, inv_parts[g], inv_full)
            lse_full = jnp.where(vmask, lse_parts[g], lse_full)
        o_ref[pl.ds(r0, blk), :] = (o * inv_full).astype(o_ref.dtype)
        if with_lse:
            lse_ref[pl.ds(r0, blk), :] = lse_full
        return carry

    lax.fori_loop(0, n // blk, body, 0)


def _banded_attention(slopes, sink, z, pos, *, dil, radius, groups, q_col, q_w, k_col, k_w, v_col,
                      with_lse, out_dtype, name):
    bsz, t, zw = z.shape
    n = t // dil
    blk = BAND_BLK
    zf = z.reshape(bsz, n, dil * zw)
    pos_f = pos.reshape(bsz, n, dil).transpose(0, 2, 1)
    pos_col = pos_f.reshape(bsz, dil, n, 1)
    pos_row = pos_f.reshape(bsz, dil, n // blk, 1, blk)
    kern = functools.partial(_banded_kernel, n=n, radius=radius, groups=groups, with_lse=with_lse)
    o_spec = pl.BlockSpec((None, n, GROUP_W), lambda b, r: (b, 0, r))
    out_specs = [o_spec]
    out_shape = [jax.ShapeDtypeStruct((bsz, n, dil * GROUP_W), out_dtype)]
    if with_lse:
        out_specs.append(o_spec)
        out_shape.append(jax.ShapeDtypeStruct((bsz, n, dil * GROUP_W), F32))
    smem = pl.BlockSpec(memory_space=pltpu.SMEM)

    def zspec(col, w):
        return pl.BlockSpec((None, n, w), lambda b, r: (b, 0, r * (zw // w) + col // w))

    outs = pl.pallas_call(
        kern,
        grid=(bsz, dil),
        in_specs=[smem, smem, zspec(q_col, q_w), zspec(k_col, k_w), zspec(v_col, GROUP_W),
                  pl.BlockSpec((None, None, n, 1), lambda b, r: (b, r, 0, 0)),
                  pl.BlockSpec((None, None, n // blk, 1, blk), lambda b, r: (b, r, 0, 0, 0))],
        out_specs=out_specs,
        out_shape=out_shape,
        scratch_shapes=[pltpu.VMEM((len(groups), n, GROUP_W), BF16)],
        compiler_params=_cparams(("arbitrary", "arbitrary")),
        name=name,
    )(slopes, sink, zf, zf, zf, pos_col, pos_row)
    return [o.reshape(bsz, t, GROUP_W) for o in outs]


def _cat_lanes(xs):
    return xs[0] if len(xs) == 1 else jnp.concatenate(xs, axis=1)


def _flash_loop(n_kb, scores_fn, vs_fn, n_maps, tq, tk, s_refs, p_refs, acc_ref, fix_fn=None):
    n = n_maps * tq
    n_g = 2 * n_maps
    n_l = V_ROWS - 2 * HEAD_DIM
    hd = HEAD_DIM
    pv_w = 2 * LANE
    n_pv = n // pv_w
    row_l = lax.broadcasted_iota(jnp.int32, (n_l, 1), 0)

    def scores_part(tile_fn, s_ref, g):
        s = tile_fn()
        s_ref[:, g * tq:(g + 1) * tq] = s
        return jnp.max(s, axis=0, keepdims=True)

    def softmax_part(s_ref, p_ref, m, col_max, g):
        mn = jnp.maximum(m[g], col_max[g])
        h, mp = divmod(g, n_maps)
        p_ref[h * tk:(h + 1) * tk, mp * tq:(mp + 1) * tq] = (
            jnp.exp2(s_ref[:, g * tq:(g + 1) * tq] - mn).astype(BF16))
        return mn, jnp.exp2(m[g] - mn)

    def pv_part(vs, p_ref, alphas, j):
        c = slice(j * pv_w, (j + 1) * pv_w)
        pv = _dot(vs, p_ref[:, c])
        acc_ref[0:hd, c] = alphas[0][:, c] * acc_ref[0:hd, c] + pv[0:hd]
        acc_ref[hd:2 * hd, c] = alphas[1][:, c] * acc_ref[hd:2 * hd, c] + pv[hd:2 * hd]
        acc_ref[2 * hd:V_ROWS, c] = alphas[2][:, c] * acc_ref[2 * hd:V_ROWS, c] + pv[2 * hd:V_ROWS]

    def pack_alphas(alphas):
        a0 = _cat_lanes(alphas[0:n_maps])
        a1 = _cat_lanes(alphas[n_maps:n_g])
        return a0, a1, jnp.where(row_l == 0, a0, a1)

    def scores_stage(kb, s_ref):
        tiles = scores_fn(kb)
        col_max = [scores_part(tiles[g], s_ref, g) for g in range(n_g)]
        return col_max if fix_fn is None else fix_fn(kb, s_ref, col_max)

    def step(kb, cur, m, col_max, a_prev):
        nxt = 1 - cur
        kb_next = jnp.minimum(kb + 1, n_kb - 1)
        tiles = scores_fn(kb_next)
        vs = vs_fn(jnp.maximum(kb - 1, 0))
        m_new, alphas, col_max_next = [], [], []
        for g in range(n_g):
            mn, alpha = softmax_part(s_refs[cur], p_refs[cur], m, col_max, g)
            m_new.append(mn)
            alphas.append(alpha)
            col_max_next.append(scores_part(tiles[g], s_refs[nxt], g))
            if (g + 1) * n_pv % n_g == 0:
                pv_part(vs, p_refs[nxt], a_prev, (g + 1) * n_pv // n_g - 1)
        if fix_fn is not None:
            col_max_next = fix_fn(kb_next, s_refs[nxt], col_max_next)
        return m_new, col_max_next, pack_alphas(alphas)

    def body(i, carry):
        m, col_max, a_prev = carry
        m, col_max, a_prev = step(2 * i, 0, m, col_max, a_prev)
        return step(2 * i + 1, 1, m, col_max, a_prev)

    acc_ref[...] = jnp.zeros((V_ROWS, n), F32)
    p_refs[1][...] = jnp.zeros((2 * tk, n), BF16)
    init = ([jnp.full((1, tq), NEG, F32)] * n_g, scores_stage(0, s_refs[0]),
            (jnp.ones((1, n), F32), jnp.ones((1, n), F32), jnp.ones((n_l, n), F32)))
    _, _, a_last = lax.fori_loop(0, n_kb // 2, body, init)
    vs_last = vs_fn(n_kb - 1)
    for j in range(n_pv):
        pv_part(vs_last, p_refs[1], a_last, j)
    return acc_ref[0:hd], acc_ref[hd:2 * hd], acc_ref[2 * hd:V_ROWS]


def _flash_scratch(n_maps, tq, tk):
    n = n_maps * tq
    return ([pltpu.VMEM((tk, 2 * n), F32)] * 2 + [pltpu.VMEM((2 * tk, n), BF16)] * 2
            + [pltpu.VMEM((V_ROWS, n), F32)])


def _fill_vt(v_ref, vt_ref, t, tk):
    v_t = v_ref[0].astype(F32).T
    row = lax.broadcasted_iota(jnp.int32, (V_ROWS - 2 * HEAD_DIM, tk), 0)
    zeros = jnp.zeros((HEAD_DIM, tk), F32)
    for kb in range(t // tk):
        blk = v_t[:, kb * tk:(kb + 1) * tk]
        for h in range(2):
            top = blk[0:HEAD_DIM] if h == 0 else zeros
            bot = zeros if h == 0 else blk[HEAD_DIM:2 * HEAD_DIM]
            ones = jnp.where(row == h, 1.0, 0.0)
            vt_ref[h, kb] = jnp.concatenate([top, bot, ones], axis=0).astype(BF16)


def _split3(x):
    x1 = x.astype(BF16).astype(F32)
    x2 = (x - x1).astype(BF16).astype(F32)
    x3 = (x - x1 - x2).astype(BF16).astype(F32)
    return x1, x2, x3


def _diff_kernel(sl_ref, base_ref, qlo_ref, qhi_ref, klo_ref, khi_ref, lq1_ref, lk1_ref, lq2_ref, lk2_ref,
                 gd_ref, q_ref, k_ref, v_ref, pq_ref, pk_ref, o_ref, vt_ref, kx_ref,
                 s0_ref, s1_ref, p0_ref, p1_ref, acc_ref, *, t, lam_init):
    tq, tk = TQ_DIFF, TK_DENSE
    n_q, n_k = t // tq, t // tk
    b = pl.program_id(0)
    hp = pl.program_id(1)
    qi = pl.program_id(2)
    base = base_ref[b]
    base_f = base.astype(F32)
    lane = _lane_iota(LANE)

    @pl.when(qi == 0)
    def _():
        _fill_vt(v_ref, vt_ref, t, tk)
        p_rel = pk_ref[0] - base_f
        p_hi = jnp.floor(p_rel * (1.0 / FOLD_LO)) * FOLD_LO
        p_lo = p_rel - p_hi
        aug = jnp.where(lane < 3, p_hi, jnp.where(lane < 6, p_lo, jnp.where(lane < 9, 1.0, 0.0)))
        kx_ref[:, 0:LANE] = k_ref[0]
        kx_ref[:, LANE:2 * LANE] = aug.astype(BF16)

    q_t = q_ref[0].astype(F32).T
    row = lax.broadcasted_iota(jnp.int32, (LANE, 1), 0)
    qs_t = jnp.concatenate(
        [jnp.where((row >= C_QK * g) & (row < C_QK * (g + 1)), q_t, 0.0) for g in range(4)],
        axis=1).astype(BF16)
    pq = pq_ref[0, 0]
    slopes2 = [sl_ref[N_HEADS + 2 * hp] * LOG2E, sl_ref[N_HEADS + 2 * hp + 1] * LOG2E]
    arow = lax.broadcasted_iota(jnp.int32, (AUG_ROWS, 1), 0)
    aug_parts = []
    for h in range(2):
        m1, m2, m3 = _split3(jnp.full((1, tq), slopes2[h], F32))
        w1, w2, w3 = _split3(-slopes2[h] * (pq - base_f))
        blk = jnp.where((arow == 0) | (arow == 3), m1,
              jnp.where((arow == 1) | (arow == 4), m2,
              jnp.where((arow == 2) | (arow == 5), m3,
              jnp.where(arow == 6, w1, jnp.where(arow == 7, w2, jnp.where(arow == 8, w3, 0.0))))))
        aug_parts += [blk, blk]
    aug_plus = jnp.concatenate(aug_parts, axis=1)
    pad_rows = jnp.zeros((LANE - AUG_ROWS, 4 * tq), BF16)
    q_hi = qhi_ref[b * n_q + qi]
    q_lo = qlo_ref[b * n_q + qi]
    q_ok = q_hi - base < FOLD_RANGE

    def sign_of(kb):
        k_lo = klo_ref[b * n_k + kb]
        k_hi = khi_ref[b * n_k + kb]
        ok = q_ok & (k_hi - base < FOLD_RANGE)
        return jnp.where(ok & (k_hi <= q_lo), 1, jnp.where(ok & (k_lo >= q_hi), -1, 0))

    def scores(kb):
        off = pl.multiple_of(kb * tk, tk)
        kk = kx_ref[pl.ds(off, tk), :]
        sg = sign_of(kb).astype(F32)
        rhs = jnp.concatenate([qs_t, (sg * aug_plus).astype(BF16), pad_rows], axis=0)
        return [functools.partial(lambda g: _dot(kk, rhs[:, g * tq:(g + 1) * tq]), g)
                for g in range(4)]

    def fix(kb, s_ref, col_max):
        def explicit():
            off = pl.multiple_of(kb * tk, tk)
            pk = pk_ref[0, pl.ds(off, tk), :]
            dist = jnp.abs(jnp.concatenate([pk] * (tq // LANE), axis=1) - pq)
            out = []
            for g in range(4):
                s = s_ref[:, g * tq:(g + 1) * tq] - slopes2[g // 2] * dist
                s_ref[:, g * tq:(g + 1) * tq] = s
                out.append(jnp.max(s, axis=0, keepdims=True))
            return out
        return lax.cond(sign_of(kb) == 0, explicit, lambda: col_max)

    def values(kb):
        return jnp.concatenate([vt_ref[0, kb], vt_ref[1, kb]], axis=1)

    acc0, acc1, acc_l = _flash_loop(n_k, scores, values, 2, tq, tk,
                                    (s0_ref, s1_ref), (p0_ref, p1_ref), acc_ref, fix_fn=fix)
    lam =(jnp.exp(jnp.sum(lq1_ref[...] * lk1_ref[...], axis=1, keepdims=True))
           - jnp.exp(jnp.sum(lq2_ref[...] * lk2_ref[...], axis=1, keepdims=True)) + lam_init)
    o0 = acc0 / acc_l[0:1]
    o1 = acc1 / acc_l[1:2]
    o_t = jnp.concatenate([o0[:, 0:tq] - lam * o0[:, tq:2 * tq],
                           o1[:, 0:tq] - lam * o1[:, tq:2 * tq]], axis=0)
    o = o_t.T
    lane = _lane_iota(LANE)
    o_sq = o * o
    ms0 = jnp.sum(jnp.where(lane < HEAD_DIM, o_sq, 0.0), axis=1, keepdims=True) * (1.0 / HEAD_DIM)
    ms1 = jnp.sum(jnp.where(lane >= HEAD_DIM, o_sq, 0.0), axis=1, keepdims=True) * (1.0 / HEAD_DIM)
    ms = jnp.where(lane < HEAD_DIM, ms0, ms1)
    y = o * lax.rsqrt(ms + EPS) * gd_ref[...]
    o_ref[0] = (y * (1.0 - lam_init)).astype(o_ref.dtype)


def _block_stats(positions):
    bsz, t = positions.shape
    pq = positions.reshape(bsz, t // TQ_DIFF, TQ_DIFF)
    pk = positions.reshape(bsz, t // TK_DENSE, TK_DENSE)
    return (jnp.min(positions, axis=1), jnp.min(pq, axis=2).reshape(-1), jnp.max(pq, axis=2).reshape(-1),
            jnp.min(pk, axis=2).reshape(-1), jnp.max(pk, axis=2).reshape(-1))


def _diff_attention(slopes, stats, lq1, lk1, lq2, lk2, g_diff2, z, pos_q_rows, pos_k_lanes, layer):
    bsz, t, _ = z.shape
    tq, tk = TQ_DIFF, TK_DENSE
    lam_init = 0.8 - 0.6 * math.exp(-0.3 * layer)
    kern = functools.partial(_diff_kernel, t=t, lam_init=lam_init)
    lsel = lambda b, h, i: (layer, 0, 0)
    smem = pl.BlockSpec(memory_space=pltpu.SMEM)
    return pl.pallas_call(
        kern,
        grid=(bsz, 2, t // tq),
        in_specs=[smem] * 6 +
                 [pl.BlockSpec((None, 1, C_QK), lsel), pl.BlockSpec((None, 1, C_QK), lsel),
                  pl.BlockSpec((None, 1, C_QK), lsel), pl.BlockSpec((None, 1, C_QK), lsel),
                  pl.BlockSpec((None, 1, LANE), lsel),
                  pl.BlockSpec((1, tq, LANE), lambda b, h, i: (b, i, Z_QC // LANE + h)),
                  pl.BlockSpec((1, t, LANE), lambda b, h, i: (b, 0, Z_KC // LANE + h)),
                  pl.BlockSpec((1, t, LANE), lambda b, h, i: (b, 0, Z_VC // LANE + h)),
                  pl.BlockSpec((1, 1, 1, tq), lambda b, h, i: (b, i, 0, 0)),
                  pl.BlockSpec((1, t, LANE), lambda b, h, i: (b, 0, 0))],
        out_specs=pl.BlockSpec((1, tq, LANE), lambda b, h, i: (b, i, h)),
        out_shape=jax.ShapeDtypeStruct((bsz, t, GROUP_W), BF16),
        scratch_shapes=[pltpu.VMEM((2, t // tk, V_ROWS, tk), BF16), pltpu.VMEM((t, 2 * LANE), BF16)]
                       + _flash_scratch(2, tq, tk),
        compiler_params=_cparams(("arbitrary", "arbitrary", "arbitrary")),
        name="diff_attention",
    )(slopes, *stats, lq1, lk1, lq2, lk2, g_diff2, z, z, z, pos_q_rows, pos_k_lanes)


def _mla_kernel(q_ref, k_ref, v_ref, o_ref, vt_ref, s0_ref, s1_ref, p0_ref, p1_ref, acc_ref, *, t):
    tq, tk = TQ_MLA, TK_DENSE

    @pl.when(pl.program_id(2) == 0)
    def _():
        _fill_vt(v_ref, vt_ref, t, tk)

    q = q_ref[0].astype(F32)
    q_t = [q[:, h * LANE:(h + 1) * LANE].T.astype(BF16) for h in range(2)]

    def scores(kb):
        off = pl.multiple_of(kb * tk, tk)
        kk = k_ref[0, pl.ds(off, tk), :]
        return [functools.partial(lambda h: _dot(kk[:, h * LANE:(h + 1) * LANE], q_t[h]), h)
                for h in range(2)]

    def values(kb):
        return jnp.concatenate([vt_ref[0, kb], vt_ref[1, kb]], axis=1)

    acc0, acc1, acc_l = _flash_loop(t // tk, scores, values, 1, tq, tk,
                                    (s0_ref, s1_ref), (p0_ref, p1_ref), acc_ref)
    o_t = jnp.concatenate([acc0 / acc_l[0:1], acc1 / acc_l[1:2]], axis=0)
    o_ref[0] = o_t.T.astype(o_ref.dtype)


def _mla_attention(z):
    bsz, t, _ = z.shape
    tq, tk = TQ_MLA, TK_DENSE
    kern = functools.partial(_mla_kernel, t=t)
    return pl.pallas_call(
        kern,
        grid=(bsz, 2, t // tq),
        in_specs=[pl.BlockSpec((1, tq, 2 * LANE), lambda b, h, i: (b, i, Z_QD // 256 + h)),
                  pl.BlockSpec((1, t, 2 * LANE), lambda b, h, i: (b, 0, Z_KD // 256 + h)),
                  pl.BlockSpec((1, t, LANE), lambda b, h, i: (b, 0, Z_VD // LANE + h))],
        out_specs=pl.BlockSpec((1, tq, LANE), lambda b, h, i: (b, i, h)),
        out_shape=jax.ShapeDtypeStruct((bsz, t, GROUP_W), BF16),
        scratch_shapes=[pltpu.VMEM((2, t // tk, V_ROWS, tk), BF16)] + _flash_scratch(1, tq, tk),
        compiler_params=_cparams(("arbitrary", "arbitrary", "arbitrary")),
        name="mla_attention",
    )(z, z, z)


def _mix_ffn_kernel(x_ref, oa1_ref, oa2_ref, oa3_ref, la1_ref, la2_ref, la3_ref, ob_ref, oc_ref, od_ref,
                    mod_ref, gpm_ref, gpf_ref, gqf_ref, wout_ref, wgu_ref, wd_ref, out_ref, a_ref):
    d = D_MODEL
    x = x_ref[0]
    mod = mod_ref[0]
    gate_m = mod[:, 2 * d:3 * d]
    shift_f, scale_f, gate_f = mod[:, 3 * d:4 * d], mod[:, 4 * d:5 * d], mod[:, 5 * d:6 * d]
    l1, l2, l3 = la1_ref[0], la2_ref[0], la3_ref[0]
    mx = jnp.maximum(jnp.maximum(l1, l2), l3)
    e1, e2, e3 = jnp.exp(l1 - mx), jnp.exp(l2 - mx), jnp.exp(l3 - mx)
    mix_a = (e1 * oa1_ref[0] + e2 * oa2_ref[0] + e3 * oa3_ref[0]) / (e1 + e2 + e3)
    mix = jnp.concatenate([mix_a.astype(BF16), ob_ref[0], oc_ref[0], od_ref[0]], axis=1)
    y = _dot(mix, wout_ref[...])
    x1 = x + gate_m * _rms(y, gpm_ref[...])
    h = (_rms(x1, gpf_ref[...]) * (1.0 + scale_f) + shift_f).astype(BF16)
    ch = 256
    for c0 in range(0, D_FF, ch):
        g = _dot(h, wgu_ref[:, c0:c0 + ch])
        u = _dot(h, wgu_ref[:, D_FF + c0:D_FF + c0 + ch])
        a_ref[:, c0:c0 + ch] = (g / (1.0 + jnp.exp(-g)) * u).astype(BF16)
    y2 = _dot(a_ref[...], wd_ref[...])
    out_ref[0] = x1 + gate_f * _rms(y2, gqf_ref[...])


def _mix_ffn(x, oa, la, ob, oc, od, mod, g_post_mix, g_pre_ffn, g_post_ffn, w_out, w_gu, w_down, layer):
    bsz, t, d = x.shape
    tm = TM_PROJ
    tok = lambda w: pl.BlockSpec((1, tm, w), lambda b, i: (b, i, 0))
    lsel = lambda b, i: (layer, 0, 0)
    once = dict(pipeline_mode=pl.Buffered(1))
    return pl.pallas_call(
        _mix_ffn_kernel,
        grid=(bsz, t // tm),
        in_specs=[tok(d)] + [tok(GROUP_W)] * 9 +
                 [pl.BlockSpec((None, 1, 1, 6 * d), lambda b, i: (layer, b, 0, 0)),
                  pl.BlockSpec((None, 1, d), lsel), pl.BlockSpec((None, 1, d), lsel),
                  pl.BlockSpec((None, 1, d), lsel),
                  pl.BlockSpec((None, d, d), lsel, **once),
                  pl.BlockSpec((None, d, 2 * D_FF), lsel, **once),
                  pl.BlockSpec((None, D_FF, d), lsel, **once)],
        out_specs=tok(d),
        out_shape=jax.ShapeDtypeStruct((bsz, t, d), F32),
        scratch_shapes=[pltpu.VMEM((tm, D_FF), BF16)],
        compiler_params=_cparams(("arbitrary", "arbitrary")),
        name="mix_ffn",
    )(x, *oa, *la, ob, oc, od, mod, g_post_mix, g_pre_ffn, g_post_ffn, w_out, w_gu, w_down)


def _rot_cols(w):
    half = w.shape[-1] // 2
    return jnp.concatenate([-w[..., half:], w[..., :half]], axis=-1)


def _prep_w_in(w_in):
    n_layers, d, _ = w_in.shape
    z64 = jnp.zeros((n_layers, d, 64), w_in.dtype)
    z32 = jnp.zeros((n_layers, d, 32), w_in.dtype)
    col = lambda a, b: w_in[:, :, a:b]
    kr = col(2560, 2592)
    qb = 768
    vb = 1152
    parts = [col(2048, 2432), col(2432, 2560),
             z64, kr, z32,
             z64, _rot_cols(kr), z32,
             col(0, 768),
             col(qb, qb + 64), col(qb + 128, qb + 192), col(qb + 64, qb + 128), col(qb + 192, qb + 256),
             col(vb, vb + 64), col(vb, vb + 64), col(vb + 64, vb + 128), col(vb + 64, vb + 128),
             col(1280, 2048),
             col(1024, 1152)]
    return jnp.concatenate(parts, axis=-1).astype(BF16)


def _prep_w_uq(w_uq):
    n_layers, r, _ = w_uq.shape
    z64 = jnp.zeros((n_layers, r, 64), w_uq.dtype)
    z32 = jnp.zeros((n_layers, r, 32), w_uq.dtype)
    main, rot = [], []
    for h in range(N_HEADS):
        base = h * (HEAD_DIM + D_ROPE)
        nope = w_uq[:, :, base:base + HEAD_DIM]
        rope = w_uq[:, :, base + HEAD_DIM:base + HEAD_DIM + D_ROPE]
        main += [nope, rope, z32]
        rot += [z64, _rot_cols(rope), z32]
    return jnp.concatenate(main + rot, axis=-1).astype(BF16)


def _prep_w_ukv(w_ukv):
    n_layers, r, _ = w_ukv.shape
    z64 = jnp.zeros((n_layers, r, 64), w_ukv.dtype)
    ks, vs = [], []
    for h in range(N_HEADS):
        base = h * 2 * HEAD_DIM
        ks += [w_ukv[:, :, base:base + HEAD_DIM], z64]
        vs.append(w_ukv[:, :, base + HEAD_DIM:base + 2 * HEAD_DIM])
    return jnp.concatenate(ks + vs, axis=-1).astype(BF16)


def _col_scale():
    cs = np.ones((1, RAW_PASS), np.float32)
    cs[0, Z_QA - Z_QA:Z_QA - Z_QA + 256] = HEAD_DIM ** -0.5
    cs[0, Z_QB - Z_QA:Z_QB - Z_QA + 256] = HEAD_DIM ** -0.5
    cs[0, Z_QC - Z_QA:Z_QC - Z_QA + 256] = C_QK ** -0.5 * LOG2E
    return jnp.asarray(cs)


_A_GROUPS = tuple((0, GROUP_W, 64 * h, 64 * h + 64, 64 * h, 64 * h + 64, 8 + h, None) for h in range(4))
_B_GROUPS = tuple((128 * r, 128, 64 * g, 64 * g + 64, 128 * g + 64 * r, 128 * g + 64 * r + 64, 2 * g + r, 2 * g + r)
                  for g in range(2) for r in range(2))


def kernel(x, c, positions, w_ada, b_ada, g_pre_mix, g_post_mix, w_in, sink_logits, lam_q1, lam_k1, lam_q2, lam_k2, g_diff, g_mla_q, g_mla_kv, w_uq, w_ukv, w_out, g_pre_ffn, g_post_ffn, w_gate_up, w_down):
    bsz, t, d = x.shape
    n_layers = w_in.shape[0]
    pos = positions.astype(F32)
    pos_col = pos.reshape(bsz, t, 1)
    pos_q_rows = pos.reshape(bsz, t // TQ_DIFF, 1, TQ_DIFF)
    pos_k_lanes = jnp.broadcast_to(pos[:, :, None], (bsz, t, LANE))
    stats = _block_stats(positions)
    j = jnp.arange(1, N_ALIBI + 1, dtype=F32)
    slopes = jnp.exp2(-8.0 * j / N_ALIBI)
    half = D_ROPE // 2
    inv = jnp.power(ROPE_THETA, -jnp.arange(half, dtype=F32) / half)
    inv_full = jnp.concatenate([jnp.zeros((64,), F32), inv, inv, jnp.zeros((32,), F32)]).reshape(1, LANE)

    w1 = _prep_w_in(w_in)
    wq2 = _prep_w_uq(w_uq)
    wkv2 = _prep_w_ukv(w_ukv)
    w_out_b = w_out.astype(BF16)
    w_gu_b = w_gate_up.astype(BF16)
    w_down_b = w_down.astype(BF16)
    col_scale = _col_scale()
    r3 = lambda a: a.reshape(n_layers, 1, a.shape[-1])
    g_diff2 = r3(jnp.concatenate([g_diff, g_diff], axis=-1))
    no_sink = jnp.zeros((1,), F32)

    mod = _ada_modulation(c, w_ada, b_ada).reshape(n_layers, bsz, 1, 6 * d)
    cos_t, sin_t = _rope_tables(pos_col, inv_full)

    for layer in range(n_layers):
        z = _input_projection(x, mod, r3(g_pre_mix), w1, col_scale, r3(g_mla_q), r3(g_mla_kv),
                              wq2, wkv2, cos_t, sin_t, layer)
        oa, la = [], []
        z_a = z[:, :, Z_QA:Z_QA + 3 * GROUP_W]
        for dil in DILATIONS:
            src, col0 = (z, Z_QA) if dil == 1 else (z_a, 0)
            o_i, l_i = _banded_attention(slopes, no_sink, src, pos, dil=dil, radius=A_RADIUS, groups=_A_GROUPS,
                                         q_col=col0, q_w=GROUP_W, k_col=col0 + GROUP_W, k_w=GROUP_W,
                                         v_col=col0 + 2 * GROUP_W,
                                         with_lse=True, out_dtype=F32, name="dilated_attention_%d" % dil)
            oa.append(o_i)
            la.append(l_i)
        (ob,) = _banded_attention(slopes, sink_logits[layer], z, pos, dil=1, radius=B_RADIUS, groups=_B_GROUPS,
                                  q_col=Z_QB, q_w=GROUP_W, k_col=Z_KB, k_w=LANE, v_col=Z_VB,
                                  with_lse=False, out_dtype=BF16, name="windowed_attention")
        oc = _diff_attention(slopes, stats, r3(lam_q1), r3(lam_k1), r3(lam_q2), r3(lam_k2), g_diff2, z,
                             pos_q_rows, pos_k_lanes, layer)
        od = _mla_attention(z)
        x = _mix_ffn(x, oa, la, ob, oc, od, mod, r3(g_post_mix), r3(g_pre_ffn), r3(g_post_ffn),
                     w_out_b, w_gu_b, w_down_b, layer)
    return x
```

```python
import functools
import math

import numpy as np
import jax
import jax.numpy as jnp
from jax import lax
from jax.experimental import pallas as pl
from jax.experimental.pallas import tpu as pltpu

F32 = jnp.float32
BF16 = jnp.bfloat16

D_MODEL = 1024
HEAD_DIM = 64
N_HEADS = 4
GROUP_W = N_HEADS * HEAD_DIM
DILATIONS = (1, 4, 16)
A_RADIUS = 64
B_RADIUS = 128
C_QK = 32
D_Q_RANK = 384
D_KV_RANK = 128
D_ROPE = 32
D_FF = 2816
ROPE_THETA = 10000.0
N_ALIBI = 12
EPS = 1e-6
NEG = -1e30
LANE = 128

RAW_D = 768
RAW_PASS = 2176
RAW_W = RAW_D + RAW_PASS
Z_QD, Z_KD, Z_VD = 0, 512, 1024
Z_QA, Z_KA, Z_VA = 1280, 1536, 1792
Z_QB, Z_VB = 2048, 2304
Z_QC, Z_KC, Z_VC = 2560, 2816, 3072
Z_KB = 3328
Z_PAD = 3456
Z_W = 3584

TM_PROJ = 512
TQ_DIFF = 256
TQ_MLA = 512
TK_DENSE = 512
V_ROWS = 2 * HEAD_DIM + 16
LOG2E = 1.4426950408889634
FOLD_LO = 64.0
FOLD_RANGE = 16384
AUG_ROWS = 16
BAND_BLK = 128
BAND_UNROLL = 4
VMEM_LIMIT = 56 * 1024 * 1024


def _cparams(sem):
    return pltpu.CompilerParams(dimension_semantics=sem, vmem_limit_bytes=VMEM_LIMIT)


def _rms(x, g):
    ms = jnp.mean(x * x, axis=-1, keepdims=True)
    return x * lax.rsqrt(ms + EPS) * g


def _dot(a, b):
    return jnp.dot(a, b, preferred_element_type=F32)


def _dot_nt(a, b):
    return lax.dot_general(a, b, (((1,), (1,)), ((), ())), preferred_element_type=F32)


def _lane_iota(w):
    return lax.broadcasted_iota(jnp.int32, (1, w), 1)


def _lane_mask(w, lo, hi):
    lane = _lane_iota(w)
    return (lane >= lo) & (lane < hi)


def _ada_kernel(c_ref, w_ref, b_ref, o_ref):
    c = c_ref[...]
    c_act = c / (1.0 + jnp.exp(-c))
    o_ref[0] = jnp.dot(c_act, w_ref[0], precision=lax.Precision.HIGHEST,
                       preferred_element_type=F32) + b_ref[0]


def _ada_modulation(c, w_ada, b_ada):
    n_layers, d, six_d = w_ada.shape
    bsz = c.shape[0]
    tn = 1024
    return pl.pallas_call(
        _ada_kernel,
        grid=(n_layers, six_d // tn),
        in_specs=[pl.BlockSpec((bsz, d), lambda l, j: (0, 0)),
                  pl.BlockSpec((1, d, tn), lambda l, j: (l, 0, j)),
                  pl.BlockSpec((1, 1, tn), lambda l, j: (l, 0, j))],
        out_specs=pl.BlockSpec((1, bsz, tn), lambda l, j: (l, 0, j)),
        out_shape=jax.ShapeDtypeStruct((n_layers, bsz, six_d), F32),
        compiler_params=_cparams(("arbitrary", "arbitrary")),
        name="ada_modulation",
    )(c, w_ada, b_ada.reshape(n_layers, 1, six_d))


def _rope_kernel(p_ref, inv_ref, cos_ref, sin_ref):
    ang = p_ref[0] * inv_ref[...]
    cos_ref[0] = jnp.cos(ang)
    sin_ref[0] = jnp.sin(ang)


def _rope_tables(pos_col, inv_full):
    bsz, t, _ = pos_col.shape
    tm = 1024
    spec = pl.BlockSpec((1, tm, LANE), lambda b, i: (b, i, 0))
    return pl.pallas_call(
        _rope_kernel,
        grid=(bsz, t // tm),
        in_specs=[pl.BlockSpec((1, tm, 1), lambda b, i: (b, i, 0)),
                  pl.BlockSpec((1, LANE), lambda b, i: (0, 0))],
        out_specs=[spec, spec],
        out_shape=[jax.ShapeDtypeStruct((bsz, t, LANE), F32)] * 2,
        compiler_params=_cparams(("arbitrary", "arbitrary")),
        name="rope_tables",
    )(pos_col, inv_full)


def _inproj_kernel(x_ref, mod_ref, g_ref, w1_ref, cs_ref, gq_ref, gkv_ref, wq2_ref, wkv2_ref,
                   cos_ref, sin_ref, z_ref, zf4_ref, zf16_ref, fold_ref, *, d_scale):
    x = x_ref[0]
    tm = x.shape[0]
    mod = mod_ref[0]
    shift, scale = mod[:, 0:D_MODEL], mod[:, D_MODEL:2 * D_MODEL]
    h = (_rms(x, g_ref[...]) * (1.0 + scale) + shift).astype(BF16)
    raw = _dot(h, w1_ref[...])
    cq = raw[:, 0:D_Q_RANK]
    ckv = raw[:, D_Q_RANK:D_Q_RANK + D_KV_RANK]
    kr_a = raw[:, 512:640]
    kr_b = raw[:, 640:768]
    cos = cos_ref[0]
    sin = sin_ref[0]
    cos4 = jnp.concatenate([cos] * N_HEADS, axis=1)
    sin4 = jnp.concatenate([sin] * N_HEADS, axis=1)
    q2 = _dot(_rms(cq, gq_ref[...]).astype(BF16), wq2_ref[...])
    qd = (q2[:, 0:512] * cos4 + q2[:, 512:1024] * sin4) * d_scale
    kv2 = _dot(_rms(ckv, gkv_ref[...]).astype(BF16), wkv2_ref[...])
    kr = kr_a * cos + kr_b * sin
    kd = kv2[:, 0:512] + jnp.concatenate([kr] * N_HEADS, axis=1)
    z_ref[0, :, Z_QD:Z_QD + 512] = qd.astype(BF16)
    z_ref[0, :, Z_KD:Z_KD + 512] = kd.astype(BF16)
    z_ref[0, :, Z_VD:Z_VD + 256] = kv2[:, 512:768].astype(BF16)
    passed = raw[:, RAW_D:RAW_W] * cs_ref[...]
    z_ref[0, :, Z_QA:Z_PAD] = passed.astype(BF16)
    z_ref[0, :, Z_PAD:Z_W] = jnp.zeros((tm, Z_W - Z_PAD), BF16)
    n_slab = 3 * GROUP_W // LANE
    for s in range(n_slab):
        fold_ref[s] = passed[:, s * LANE:(s + 1) * LANE]
    for dil, out_ref in ((DILATIONS[1], zf4_ref), (DILATIONS[2], zf16_ref)):
        for r in range(dil):
            for s in range(n_slab):
                out_ref[0, r, :, s * LANE:(s + 1) * LANE] = (
                    fold_ref[s, pl.ds(r, tm // dil, stride=dil), :].astype(BF16))


def _input_projection(x, mod, g_pre, w1, col_scale, g_q, g_kv, wq2, wkv2, cos_t, sin_t, layer):
    bsz, t, d = x.shape
    tm = TM_PROJ
    const = lambda b, i: (0, 0)
    lsel = lambda b, i: (layer, 0, 0)
    kern = functools.partial(_inproj_kernel, d_scale=(HEAD_DIM + D_ROPE) ** -0.5 * LOG2E)
    return pl.pallas_call(
        kern,
        grid=(bsz, t // tm),
        in_specs=[pl.BlockSpec((1, tm, d), lambda b, i: (b, i, 0)),
                  pl.BlockSpec((None, 1, 1, 6 * d), lambda b, i: (layer, b, 0, 0)),
                  pl.BlockSpec((None, 1, d), lsel),
                  pl.BlockSpec((None, d, RAW_W), lsel),
                  pl.BlockSpec((1, RAW_PASS), const),
                  pl.BlockSpec((None, 1, D_Q_RANK), lsel),
                  pl.BlockSpec((None, 1, D_KV_RANK), lsel),
                  pl.BlockSpec((None, D_Q_RANK, 1024), lsel),
                  pl.BlockSpec((None, D_KV_RANK, 768), lsel),
                  pl.BlockSpec((1, tm, LANE), lambda b, i: (b, i, 0)),
                  pl.BlockSpec((1, tm, LANE), lambda b, i: (b, i, 0))],
        out_specs=[pl.BlockSpec((1, tm, Z_W), lambda b, i: (b, i, 0))] +
                  [pl.BlockSpec((1, dil, tm // dil, 3 * GROUP_W), lambda b, i: (b, 0, i, 0))
                   for dil in DILATIONS[1:]],
        out_shape=[jax.ShapeDtypeStruct((bsz, t, Z_W), BF16)] +
                  [jax.ShapeDtypeStruct((bsz, dil, t // dil, 3 * GROUP_W), BF16) for dil in DILATIONS[1:]],
        scratch_shapes=[pltpu.VMEM((3 * GROUP_W // LANE, tm, LANE), F32)],
        compiler_params=_cparams(("arbitrary", "arbitrary")),
        name="input_projection",
    )(x, mod, g_pre, w1, col_scale, g_q, g_kv, wq2, wkv2, cos_t, sin_t)


def _banded_kernel(sl_ref, sink_ref, q_ref, k_ref, v_ref, pk_ref, pq_ref, *rest,
                   n, dil, radius, groups, with_lse):
    if with_lse:
        o_ref, lse_ref, vm_ref = rest
    else:
        o_ref, vm_ref = rest
        lse_ref = None
    blk = BAND_BLK
    win = min(blk + 2 * radius, n)
    n_g = len(groups)
    v_all = v_ref[...]
    lane_o = _lane_iota(GROUP_W)
    for g, grp in enumerate(groups):
        vmask = (lane_o >= grp[4]) & (lane_o < grp[5])
        vm_ref[g] = jnp.where(vmask, v_all, jnp.zeros_like(v_all))
    stat_row = lax.broadcasted_iota(jnp.int32, (LANE, 1), 0)

    def one_block(i):
        r0 = pl.multiple_of(i * blk, blk)
        start = pl.multiple_of(jnp.clip(r0 - radius, 0, n - win), radius)
        q = q_ref[pl.ds(r0, blk), :].astype(F32)
        q_t = {}
        parts = []
        for (q_lo, q_w, qm_lo, qm_hi, _, _, _, _) in groups:
            if q_lo not in q_t:
                q_t[q_lo] = q[:, q_lo:q_lo + q_w].T
            dim = lax.broadcasted_iota(jnp.int32, (q_w, 1), 0)
            parts.append(jnp.where((dim >= qm_lo) & (dim < qm_hi), q_t[q_lo], 0.0))
        qs_t = jnp.concatenate(parts, axis=1).astype(BF16)
        s_t = _dot(k_ref[pl.ds(start, win), :], qs_t)
        dist = jnp.abs(pk_ref[pl.ds(start, win), :] - pq_ref[i])
        ki = start + lax.broadcasted_iota(jnp.int32, (win, 1), 0)
        qi = r0 + lax.broadcasted_iota(jnp.int32, (1, blk), 1)
        valid = jnp.abs(qi - ki) <= radius
        p_parts = []
        inv_rows = jnp.zeros((LANE, blk), F32)
        lse_rows = jnp.zeros((LANE, blk), F32)
        for g, grp in enumerate(groups):
            sg = s_t[:, g * blk:(g + 1) * blk] - (sl_ref[grp[6]] * LOG2E) * dist
            sg = jnp.where(valid, sg, NEG)
            m = jnp.max(sg, axis=0, keepdims=True)
            if grp[7] is not None:
                sk = sink_ref[grp[7]] * LOG2E
                m = jnp.maximum(m, sk)
            p = jnp.exp2(sg - m)
            den = jnp.sum(p, axis=0, keepdims=True)
            if grp[7] is not None:
                den = den + jnp.exp2(sk - m)
            p_parts.append(p.T.astype(BF16))
            inv_rows = jnp.where(stat_row == g, 1.0 / den, inv_rows)
            lse_rows = jnp.where(stat_row == g, m + jnp.log2(den), lse_rows)
        pcat = jnp.concatenate(p_parts, axis=1)
        vcat = jnp.concatenate([vm_ref[g, pl.ds(start, win), :] for g in range(n_g)], axis=0)
        o = _dot(pcat, vcat)
        inv_cols = inv_rows.T
        lse_cols = lse_rows.T
        inv_full = jnp.zeros((blk, GROUP_W), F32)
        lse_full = jnp.zeros((blk, GROUP_W), F32)
        for g, grp in enumerate(groups):
            vmask = (lane_o >= grp[4]) & (lane_o < grp[5])
            inv_full = jnp.where(vmask, inv_cols[:, g:g + 1], inv_full)
            lse_full = jnp.where(vmask, lse_cols[:, g:g + 1], lse_full)
        o = o * inv_full
        if not with_lse:
            o_ref[pl.ds(r0, blk), :] = o.astype(o_ref.dtype)
            return
        if dil == 1:
            rows = pl.ds(r0, blk)
        else:
            rows = pl.ds(r0 * dil + pl.program_id(1), blk, stride=dil)
        for s in range(GROUP_W // LANE):
            o_ref[s, rows, :] = o[:, s * LANE:(s + 1) * LANE]
            lse_ref[s, rows, :] = lse_full[:, s * LANE:(s + 1) * LANE]

    unroll = min(BAND_UNROLL, n // blk)

    def body(j, carry):
        for u in range(unroll):
            one_block(j * unroll + u)
        return carry

    lax.fori_loop(0, n // (blk * unroll), body, 0)


def _banded_attention(slopes, sink, src, pos, *, dil, radius, groups, q_col, q_w, k_col, k_w, v_col,
                      with_lse, name):
    bsz, t = pos.shape
    n = t // dil
    blk = BAND_BLK
    pos_f = pos.reshape(bsz, n, dil).transpose(0, 2, 1)
    pos_k = jnp.broadcast_to(pos_f[..., None], (bsz, dil, n, LANE))
    pos_q = pos_f.reshape(bsz, dil, n // blk, 1, blk)
    kern = functools.partial(_banded_kernel, n=n, dil=dil, radius=radius, groups=groups, with_lse=with_lse)
    if with_lse:
        o_spec = pl.BlockSpec((None, GROUP_W // LANE, t, LANE), lambda b, r: (b, 0, 0, 0))
        out_specs = [o_spec, o_spec]
        out_shape = [jax.ShapeDtypeStruct((bsz, GROUP_W // LANE, t, LANE), F32)] * 2
    else:
        out_specs = [pl.BlockSpec((None, t, GROUP_W), lambda b, r: (b, 0, 0))]
        out_shape = [jax.ShapeDtypeStruct((bsz, t, GROUP_W), BF16)]
    smem = pl.BlockSpec(memory_space=pltpu.SMEM)

    def zspec(col, w):
        if dil == 1:
            return pl.BlockSpec((None, n, w), lambda b, r: (b, 0, col // w))
        return pl.BlockSpec((None, None, n, w), lambda b, r: (b, r, 0, col // w))

    return pl.pallas_call(
        kern,
        grid=(bsz, dil),
        in_specs=[smem, smem, zspec(q_col, q_w), zspec(k_col, k_w), zspec(v_col, GROUP_W),
                  pl.BlockSpec((None, None, n, LANE), lambda b, r: (b, r, 0, 0)),
                  pl.BlockSpec((None, None, n // blk, 1, blk), lambda b, r: (b, r, 0, 0, 0))],
        out_specs=out_specs,
        out_shape=out_shape,
        scratch_shapes=[pltpu.VMEM((len(groups), n, GROUP_W), BF16)],
        compiler_params=_cparams(("arbitrary", "arbitrary")),
        name=name,
    )(slopes, sink, src, src, src, pos_k, pos_q)


def _cat_lanes(xs):
    return xs[0] if len(xs) == 1 else jnp.concatenate(xs, axis=1)


def _flash_loop(n_kb, scores_fn, vs_fn, n_maps, tq, tk, s_refs, p_refs, acc_ref, fix_fn=None):
    n = n_maps * tq
    n_g = 2 * n_maps
    n_l = V_ROWS - 2 * HEAD_DIM
    hd = HEAD_DIM
    pv_w = 2 * LANE
    n_pv = n // pv_w
    row_l = lax.broadcasted_iota(jnp.int32, (n_l, 1), 0)

    def scores_part(tile_fn, s_ref, g):
        s = tile_fn()
        s_ref[:, g * tq:(g + 1) * tq] = s
        return jnp.max(s, axis=0, keepdims=True)

    def softmax_part(s_ref, p_ref, m, col_max, g):
        mn = jnp.maximum(m[g], col_max[g])
        h, mp = divmod(g, n_maps)
        p_ref[h * tk:(h + 1) * tk, mp * tq:(mp + 1) * tq] = (
            jnp.exp2(s_ref[:, g * tq:(g + 1) * tq] - mn).astype(BF16))
        return mn, jnp.exp2(m[g] - mn)

    def pv_part(vs, p_ref, alphas, j):
        c = slice(j * pv_w, (j + 1) * pv_w)
        pv = _dot(vs, p_ref[:, c])
        acc_ref[0:hd, c] = alphas[0][:, c] * acc_ref[0:hd, c] + pv[0:hd]
        acc_ref[hd:2 * hd, c] = alphas[1][:, c] * acc_ref[hd:2 * hd, c] + pv[hd:2 * hd]
        acc_ref[2 * hd:V_ROWS, c] = alphas[2][:, c] * acc_ref[2 * hd:V_ROWS, c] + pv[2 * hd:V_ROWS]

    def pack_alphas(alphas):
        a0 = _cat_lanes(alphas[0:n_maps])
        a1 = _cat_lanes(alphas[n_maps:n_g])
        return a0, a1, jnp.where(row_l == 0, a0, a1)

    def scores_stage(kb, s_ref):
        tiles = scores_fn(kb)
        col_max = [scores_part(tiles[g], s_ref, g) for g in range(n_g)]
        return col_max if fix_fn is None else fix_fn(kb, s_ref, col_max)

    def step(kb, cur, m, col_max, a_prev):
        nxt = 1 - cur
        kb_next = jnp.minimum(kb + 1, n_kb - 1)
        tiles = scores_fn(kb_next)
        vs = vs_fn(jnp.maximum(kb - 1, 0))
        m_new, alphas, col_max_next = [], [], []
        for g in range(n_g):
            mn, alpha = softmax_part(s_refs[cur], p_refs[cur], m, col_max, g)
            m_new.append(mn)
            alphas.append(alpha)
            col_max_next.append(scores_part(tiles[g], s_refs[nxt], g))
            if (g + 1) * n_pv % n_g == 0:
                pv_part(vs, p_refs[nxt], a_prev, (g + 1) * n_pv // n_g - 1)
        if fix_fn is not None:
            col_max_next = fix_fn(kb_next, s_refs[nxt], col_max_next)
        return m_new, col_max_next, pack_alphas(alphas)

    def body(i, carry):
        m, col_max, a_prev = carry
        m, col_max, a_prev = step(2 * i, 0, m, col_max, a_prev)
        return step(2 * i + 1, 1, m, col_max, a_prev)

    acc_ref[...] = jnp.zeros((V_ROWS, n), F32)
    p_refs[1][...] = jnp.zeros((2 * tk, n), BF16)
    init = ([jnp.full((1, tq), NEG, F32)] * n_g, scores_stage(0, s_refs[0]),
            (jnp.ones((1, n), F32), jnp.ones((1, n), F32), jnp.ones((n_l, n), F32)))
    _, _, a_last = lax.fori_loop(0, n_kb // 2, body, init)
    vs_last = vs_fn(n_kb - 1)
    for j in range(n_pv):
        pv_part(vs_last, p_refs[1], a_last, j)
    return acc_ref[0:hd], acc_ref[hd:2 * hd], acc_ref[2 * hd:V_ROWS]


def _flash_scratch(n_maps, tq, tk):
    n = n_maps * tq
    return ([pltpu.VMEM((tk, 2 * n), F32)] * 2 + [pltpu.VMEM((2 * tk, n), BF16)] * 2
            + [pltpu.VMEM((V_ROWS, n), F32)])


def _fill_vt(v_ref, vt_ref, t, tk):
    v_t = v_ref[0].astype(F32).T
    row = lax.broadcasted_iota(jnp.int32, (V_ROWS - 2 * HEAD_DIM, tk), 0)
    zeros = jnp.zeros((HEAD_DIM, tk), F32)
    for kb in range(t // tk):
        blk = v_t[:, kb * tk:(kb + 1) * tk]
        for h in range(2):
            top = blk[0:HEAD_DIM] if h == 0 else zeros
            bot = zeros if h == 0 else blk[HEAD_DIM:2 * HEAD_DIM]
            ones = jnp.where(row == h, 1.0, 0.0)
            vt_ref[h, kb] = jnp.concatenate([top, bot, ones], axis=0).astype(BF16)


def _split3(x):
    x1 = x.astype(BF16).astype(F32)
    x2 = (x - x1).astype(BF16).astype(F32)
    x3 = (x - x1 - x2).astype(BF16).astype(F32)
    return x1, x2, x3


def _diff_kernel(sl_ref, base_ref, qlo_ref, qhi_ref, klo_ref, khi_ref, lq1_ref, lk1_ref, lq2_ref, lk2_ref,
                 gd_ref, q_ref, k_ref, v_ref, pq_ref, pk_ref, o_ref, vt_ref, kx_ref,
                 s0_ref, s1_ref, p0_ref, p1_ref, acc_ref, *, t, lam_init):
    tq, tk = TQ_DIFF, TK_DENSE
    n_q, n_k = t // tq, t // tk
    b = pl.program_id(0)
    hp = pl.program_id(1)
    qi = pl.program_id(2)
    base = base_ref[b]
    base_f = base.astype(F32)
    lane = _lane_iota(LANE)

    @pl.when(qi == 0)
    def _():
        _fill_vt(v_ref, vt_ref, t, tk)
        p_rel = pk_ref[0] - base_f
        p_hi = jnp.floor(p_rel * (1.0 / FOLD_LO)) * FOLD_LO
        p_lo = p_rel - p_hi
        aug = jnp.where(lane < 3, p_hi, jnp.where(lane < 6, p_lo, jnp.where(lane < 9, 1.0, 0.0)))
        kx_ref[:, 0:LANE] = k_ref[0]
        kx_ref[:, LANE:2 * LANE] = aug.astype(BF16)

    q_t = q_ref[0].astype(F32).T
    row = lax.broadcasted_iota(jnp.int32, (LANE, 1), 0)
    qs_t = jnp.concatenate(
        [jnp.where((row >= C_QK * g) & (row < C_QK * (g + 1)), q_t, 0.0) for g in range(4)],
        axis=1).astype(BF16)
    pq = pq_ref[0, 0]
    slopes2 = [sl_ref[N_HEADS + 2 * hp] * LOG2E, sl_ref[N_HEADS + 2 * hp + 1] * LOG2E]
    arow = lax.broadcasted_iota(jnp.int32, (AUG_ROWS, 1), 0)
    aug_parts = []
    for h in range(2):
        m1, m2, m3 = _split3(jnp.full((1, tq), slopes2[h], F32))
        w1, w2, w3 = _split3(-slopes2[h] * (pq - base_f))
        blk = jnp.where((arow == 0) | (arow == 3), m1,
              jnp.where((arow == 1) | (arow == 4), m2,
              jnp.where((arow == 2) | (arow == 5), m3,
              jnp.where(arow == 6, w1, jnp.where(arow == 7, w2, jnp.where(arow == 8, w3, 0.0))))))
        aug_parts += [blk, blk]
    aug_plus = jnp.concatenate(aug_parts, axis=1)
    pad_rows = jnp.zeros((LANE - AUG_ROWS, 4 * tq), BF16)
    q_hi = qhi_ref[b * n_q + qi]
    q_lo = qlo_ref[b * n_q + qi]
    q_ok = q_hi - base < FOLD_RANGE

    def sign_of(kb):
        k_lo = klo_ref[b * n_k + kb]
        k_hi = khi_ref[b * n_k + kb]
        ok = q_ok & (k_hi - base < FOLD_RANGE)
        return jnp.where(ok & (k_hi <= q_lo), 1, jnp.where(ok & (k_lo >= q_hi), -1, 0))

    def scores(kb):
        off = pl.multiple_of(kb * tk, tk)
        kk = kx_ref[pl.ds(off, tk), :]
        sg = sign_of(kb).astype(F32)
        rhs = jnp.concatenate([qs_t, (sg * aug_plus).astype(BF16), pad_rows], axis=0)
        return [functools.partial(lambda g: _dot(kk, rhs[:, g * tq:(g + 1) * tq]), g)
                for g in range(4)]

    def fix(kb, s_ref, col_max):
        def explicit():
            off = pl.multiple_of(kb * tk, tk)
            pk = pk_ref[0, pl.ds(off, tk), :]
            dist = jnp.abs(jnp.concatenate([pk] * (tq // LANE), axis=1) - pq)
            out = []
            for g in range(4):
                s = s_ref[:, g * tq:(g + 1) * tq] - slopes2[g // 2] * dist
                s_ref[:, g * tq:(g + 1) * tq] = s
                out.append(jnp.max(s, axis=0, keepdims=True))
            return out
        return lax.cond(sign_of(kb) == 0, explicit, lambda: col_max)

    def values(kb):
        return jnp.concatenate([vt_ref[0, kb], vt_ref[1, kb]], axis=1)

    acc0, acc1, acc_l = _flash_loop(n_k, scores, values, 2, tq, tk,
                                    (s0_ref, s1_ref), (p0_ref, p1_ref), acc_ref, fix_fn=fix)
    lam =(jnp.exp(jnp.sum(lq1_ref[...] * lk1_ref[...], axis=1, keepdims=True))
           - jnp.exp(jnp.sum(lq2_ref[...] * lk2_ref[...], axis=1, keepdims=True)) + lam_init)
    o0 = acc0 / acc_l[0:1]
    o1 = acc1 / acc_l[1:2]
    o_t = jnp.concatenate([o0[:, 0:tq] - lam * o0[:, tq:2 * tq],
                           o1[:, 0:tq] - lam * o1[:, tq:2 * tq]], axis=0)
    o = o_t.T
    lane = _lane_iota(LANE)
    o_sq = o * o
    ms0 = jnp.sum(jnp.where(lane < HEAD_DIM, o_sq, 0.0), axis=1, keepdims=True) * (1.0 / HEAD_DIM)
    ms1 = jnp.sum(jnp.where(lane >= HEAD_DIM, o_sq, 0.0), axis=1, keepdims=True) * (1.0 / HEAD_DIM)
    ms = jnp.where(lane < HEAD_DIM, ms0, ms1)
    y = o * lax.rsqrt(ms + EPS) * gd_ref[...]
    o_ref[0] = (y * (1.0 - lam_init)).astype(o_ref.dtype)


def _block_stats(positions):
    bsz, t = positions.shape
    pq = positions.reshape(bsz, t // TQ_DIFF, TQ_DIFF)
    pk = positions.reshape(bsz, t // TK_DENSE, TK_DENSE)
    return (jnp.min(positions, axis=1), jnp.min(pq, axis=2).reshape(-1), jnp.max(pq, axis=2).reshape(-1),
            jnp.min(pk, axis=2).reshape(-1), jnp.max(pk, axis=2).reshape(-1))


def _diff_attention(slopes, stats, lq1, lk1, lq2, lk2, g_diff2, z, pos_q_rows, pos_k_lanes, layer):
    bsz, t, _ = z.shape
    tq, tk = TQ_DIFF, TK_DENSE
    lam_init = 0.8 - 0.6 * math.exp(-0.3 * layer)
    kern = functools.partial(_diff_kernel, t=t, lam_init=lam_init)
    lsel = lambda b, h, i: (layer, 0, 0)
    smem = pl.BlockSpec(memory_space=pltpu.SMEM)
    return pl.pallas_call(
        kern,
        grid=(bsz, 2, t // tq),
        in_specs=[smem] * 6 +
                 [pl.BlockSpec((None, 1, C_QK), lsel), pl.BlockSpec((None, 1, C_QK), lsel),
                  pl.BlockSpec((None, 1, C_QK), lsel), pl.BlockSpec((None, 1, C_QK), lsel),
                  pl.BlockSpec((None, 1, LANE), lsel),
                  pl.BlockSpec((1, tq, LANE), lambda b, h, i: (b, i, Z_QC // LANE + h)),
                  pl.BlockSpec((1, t, LANE), lambda b, h, i: (b, 0, Z_KC // LANE + h)),
                  pl.BlockSpec((1, t, LANE), lambda b, h, i: (b, 0, Z_VC // LANE + h)),
                  pl.BlockSpec((1, 1, 1, tq), lambda b, h, i: (b, i, 0, 0)),
                  pl.BlockSpec((1, t, LANE), lambda b, h, i: (b, 0, 0))],
        out_specs=pl.BlockSpec((1, tq, LANE), lambda b, h, i: (b, i, h)),
        out_shape=jax.ShapeDtypeStruct((bsz, t, GROUP_W), BF16),
        scratch_shapes=[pltpu.VMEM((2, t // tk, V_ROWS, tk), BF16), pltpu.VMEM((t, 2 * LANE), BF16)]
                       + _flash_scratch(2, tq, tk),
        compiler_params=_cparams(("arbitrary", "arbitrary", "arbitrary")),
        name="diff_attention",
    )(slopes, *stats, lq1, lk1, lq2, lk2, g_diff2, z, z, z, pos_q_rows, pos_k_lanes)


def _mla_kernel(q_ref, k_ref, v_ref, o_ref, vt_ref, s0_ref, s1_ref, p0_ref, p1_ref, acc_ref, *, t):
    tq, tk = TQ_MLA, TK_DENSE

    @pl.when(pl.program_id(2) == 0)
    def _():
        _fill_vt(v_ref, vt_ref, t, tk)

    q = q_ref[0].astype(F32)
    q_t = [q[:, h * LANE:(h + 1) * LANE].T.astype(BF16) for h in range(2)]

    def scores(kb):
        off = pl.multiple_of(kb * tk, tk)
        kk = k_ref[0, pl.ds(off, tk), :]
        return [functools.partial(lambda h: _dot(kk[:, h * LANE:(h + 1) * LANE], q_t[h]), h)
                for h in range(2)]

    def values(kb):
        return jnp.concatenate([vt_ref[0, kb], vt_ref[1, kb]], axis=1)

    acc0, acc1, acc_l = _flash_loop(t // tk, scores, values, 1, tq, tk,
                                    (s0_ref, s1_ref), (p0_ref, p1_ref), acc_ref)
    o_t = jnp.concatenate([acc0 / acc_l[0:1], acc1 / acc_l[1:2]], axis=0)
    o_ref[0] = o_t.T.astype(o_ref.dtype)


def _mla_attention(z):
    bsz, t, _ = z.shape
    tq, tk = TQ_MLA, TK_DENSE
    kern = functools.partial(_mla_kernel, t=t)
    return pl.pallas_call(
        kern,
        grid=(bsz, 2, t // tq),
        in_specs=[pl.BlockSpec((1, tq, 2 * LANE), lambda b, h, i: (b, i, Z_QD // 256 + h)),
                  pl.BlockSpec((1, t, 2 * LANE), lambda b, h, i: (b, 0, Z_KD // 256 + h)),
                  pl.BlockSpec((1, t, LANE), lambda b, h, i: (b, 0, Z_VD // LANE + h))],
        out_specs=pl.BlockSpec((1, tq, LANE), lambda b, h, i: (b, i, h)),
        out_shape=jax.ShapeDtypeStruct((bsz, t, GROUP_W), BF16),
        scratch_shapes=[pltpu.VMEM((2, t // tk, V_ROWS, tk), BF16)] + _flash_scratch(1, tq, tk),
        compiler_params=_cparams(("arbitrary", "arbitrary", "arbitrary")),
        name="mla_attention",
    )(z, z, z)


def _mix_ffn_kernel(x_ref, oa1_ref, oa2_ref, oa3_ref, la1_ref, la2_ref, la3_ref, ob_ref, oc_ref, od_ref,
                    mod_ref, gpm_ref, gpf_ref, gqf_ref, wout_ref, wgu_ref, wd_ref, out_ref, a_ref):
    d = D_MODEL
    x = x_ref[0]
    mod = mod_ref[0]
    gate_m = mod[:, 2 * d:3 * d]
    shift_f, scale_f, gate_f = mod[:, 3 * d:4 * d], mod[:, 4 * d:5 * d], mod[:, 5 * d:6 * d]
    slabs = lambda ref: jnp.concatenate([ref[0, s] for s in range(GROUP_W // LANE)], axis=1)
    l1, l2, l3 = slabs(la1_ref), slabs(la2_ref), slabs(la3_ref)
    mx = jnp.maximum(jnp.maximum(l1, l2), l3)
    e1, e2, e3 = jnp.exp2(l1 - mx), jnp.exp2(l2 - mx), jnp.exp2(l3 - mx)
    mix_a = (e1 * slabs(oa1_ref) + e2 * slabs(oa2_ref) + e3 * slabs(oa3_ref)) / (e1 + e2 + e3)
    mix = jnp.concatenate([mix_a.astype(BF16), ob_ref[0], oc_ref[0], od_ref[0]], axis=1)
    y = _dot(mix, wout_ref[...])
    x1 = x + gate_m * _rms(y, gpm_ref[...])
    h = (_rms(x1, gpf_ref[...]) * (1.0 + scale_f) + shift_f).astype(BF16)
    ch = 256
    for c0 in range(0, D_FF, ch):
        g = _dot(h, wgu_ref[:, c0:c0 + ch])
        u = _dot(h, wgu_ref[:, D_FF + c0:D_FF + c0 + ch])
        a_ref[:, c0:c0 + ch] = (g / (1.0 + jnp.exp(-g)) * u).astype(BF16)
    y2 = _dot(a_ref[...], wd_ref[...])
    out_ref[0] = x1 + gate_f * _rms(y2, gqf_ref[...])


def _mix_ffn(x, oa, la, ob, oc, od, mod, g_post_mix, g_pre_ffn, g_post_ffn, w_out, w_gu, w_down, layer):
    bsz, t, d = x.shape
    tm = TM_PROJ
    tok = lambda w: pl.BlockSpec((1, tm, w), lambda b, i: (b, i, 0))
    lsel = lambda b, i: (layer, 0, 0)
    once = dict(pipeline_mode=pl.Buffered(1))
    return pl.pallas_call(
        _mix_ffn_kernel,
        grid=(bsz, t // tm),
        in_specs=[tok(d)] +
                 [pl.BlockSpec((1, GROUP_W // LANE, tm, LANE), lambda b, i: (b, 0, i, 0))] * 6 +
                 [tok(GROUP_W)] * 3 +
                 [pl.BlockSpec((None, 1, 1, 6 * d), lambda b, i: (layer, b, 0, 0)),
                  pl.BlockSpec((None, 1, d), lsel), pl.BlockSpec((None, 1, d), lsel),
                  pl.BlockSpec((None, 1, d), lsel),
                  pl.BlockSpec((None, d, d), lsel, **once),
                  pl.BlockSpec((None, d, 2 * D_FF), lsel, **once),
                  pl.BlockSpec((None, D_FF, d), lsel, **once)],
        out_specs=tok(d),
        out_shape=jax.ShapeDtypeStruct((bsz, t, d), F32),
        scratch_shapes=[pltpu.VMEM((tm, D_FF), BF16)],
        compiler_params=_cparams(("arbitrary", "arbitrary")),
        name="mix_ffn",
    )(x, *oa, *la, ob, oc, od, mod, g_post_mix, g_pre_ffn, g_post_ffn, w_out, w_gu, w_down)


def _rot_cols(w):
    half = w.shape[-1] // 2
    return jnp.concatenate([-w[..., half:], w[..., :half]], axis=-1)


def _prep_w_in(w_in):
    n_layers, d, _ = w_in.shape
    z64 = jnp.zeros((n_layers, d, 64), w_in.dtype)
    z32 = jnp.zeros((n_layers, d, 32), w_in.dtype)
    col = lambda a, b: w_in[:, :, a:b]
    kr = col(2560, 2592)
    qb = 768
    vb = 1152
    parts = [col(2048, 2432), col(2432, 2560),
             z64, kr, z32,
             z64, _rot_cols(kr), z32,
             col(0, 768),
             col(qb, qb + 64), col(qb + 128, qb + 192), col(qb + 64, qb + 128), col(qb + 192, qb + 256),
             col(vb, vb + 64), col(vb, vb + 64), col(vb + 64, vb + 128), col(vb + 64, vb + 128),
             col(1280, 2048),
             col(1024, 1152)]
    return jnp.concatenate(parts, axis=-1).astype(BF16)


def _prep_w_uq(w_uq):
    n_layers, r, _ = w_uq.shape
    z64 = jnp.zeros((n_layers, r, 64), w_uq.dtype)
    z32 = jnp.zeros((n_layers, r, 32), w_uq.dtype)
    main, rot = [], []
    for h in range(N_HEADS):
        base = h * (HEAD_DIM + D_ROPE)
        nope = w_uq[:, :, base:base + HEAD_DIM]
        rope = w_uq[:, :, base + HEAD_DIM:base + HEAD_DIM + D_ROPE]
        main += [nope, rope, z32]
        rot += [z64, _rot_cols(rope), z32]
    return jnp.concatenate(main + rot, axis=-1).astype(BF16)


def _prep_w_ukv(w_ukv):
    n_layers, r, _ = w_ukv.shape
    z64 = jnp.zeros((n_layers, r, 64), w_ukv.dtype)
    ks, vs = [], []
    for h in range(N_HEADS):
        base = h * 2 * HEAD_DIM
        ks += [w_ukv[:, :, base:base + HEAD_DIM], z64]
        vs.append(w_ukv[:, :, base + HEAD_DIM:base + 2 * HEAD_DIM])
    return jnp.concatenate(ks + vs, axis=-1).astype(BF16)


def _col_scale():
    cs = np.ones((1, RAW_PASS), np.float32)
    cs[0, Z_QA - Z_QA:Z_QA - Z_QA + 256] = HEAD_DIM ** -0.5 * LOG2E
    cs[0, Z_QB - Z_QA:Z_QB - Z_QA + 256] = HEAD_DIM ** -0.5 * LOG2E
    cs[0, Z_QC - Z_QA:Z_QC - Z_QA + 256] = C_QK ** -0.5 * LOG2E
    return jnp.asarray(cs)


_A_GROUPS = tuple((0, GROUP_W, 64 * h, 64 * h + 64, 64 * h, 64 * h + 64, 8 + h, None) for h in range(4))
_B_GROUPS = tuple((128 * r, 128, 64 * g, 64 * g + 64, 128 * g + 64 * r, 128 * g + 64 * r + 64, 2 * g + r, 2 * g + r)
                  for g in range(2) for r in range(2))


def kernel(x, c, positions, w_ada, b_ada, g_pre_mix, g_post_mix, w_in, sink_logits, lam_q1, lam_k1, lam_q2, lam_k2, g_diff, g_mla_q, g_mla_kv, w_uq, w_ukv, w_out, g_pre_ffn, g_post_ffn, w_gate_up, w_down):
    bsz, t, d = x.shape
    n_layers = w_in.shape[0]
    pos = positions.astype(F32)
    pos_col = pos.reshape(bsz, t, 1)
    pos_q_rows = pos.reshape(bsz, t // TQ_DIFF, 1, TQ_DIFF)
    pos_k_lanes = jnp.broadcast_to(pos[:, :, None], (bsz, t, LANE))
    stats = _block_stats(positions)
    j = jnp.arange(1, N_ALIBI + 1, dtype=F32)
    slopes = jnp.exp2(-8.0 * j / N_ALIBI)
    half = D_ROPE // 2
    inv = jnp.power(ROPE_THETA, -jnp.arange(half, dtype=F32) / half)
    inv_full = jnp.concatenate([jnp.zeros((64,), F32), inv, inv, jnp.zeros((32,), F32)]).reshape(1, LANE)

    w1 = _prep_w_in(w_in)
    wq2 = _prep_w_uq(w_uq)
    wkv2 = _prep_w_ukv(w_ukv)
    w_out_b = w_out.astype(BF16)
    w_gu_b = w_gate_up.astype(BF16)
    w_down_b = w_down.astype(BF16)
    col_scale = _col_scale()
    r3 = lambda a: a.reshape(n_layers, 1, a.shape[-1])
    g_diff2 = r3(jnp.concatenate([g_diff, g_diff], axis=-1))
    no_sink = jnp.zeros((1,), F32)

    mod = _ada_modulation(c, w_ada, b_ada).reshape(n_layers, bsz, 1, 6 * d)
    cos_t, sin_t = _rope_tables(pos_col, inv_full)

    for layer in range(n_layers):
        z, zf4, zf16 = _input_projection(x, mod, r3(g_pre_mix), w1, col_scale, r3(g_mla_q), r3(g_mla_kv),
                                         wq2, wkv2, cos_t, sin_t, layer)
        oa, la = [], []
        for dil, src, col0 in zip(DILATIONS, (z, zf4, zf16), (Z_QA, 0, 0)):
            o_i, l_i = _banded_attention(slopes, no_sink, src, pos, dil=dil, radius=A_RADIUS, groups=_A_GROUPS,
                                         q_col=col0, q_w=GROUP_W, k_col=col0 + GROUP_W, k_w=GROUP_W,
                                         v_col=col0 + 2 * GROUP_W,
                                         with_lse=True, name="dilated_attention_%d" % dil)
            oa.append(o_i)
            la.append(l_i)
        (ob,) = _banded_attention(slopes, sink_logits[layer], z, pos, dil=1, radius=B_RADIUS, groups=_B_GROUPS,
                                  q_col=Z_QB, q_w=GROUP_W, k_col=Z_KB, k_w=LANE, v_col=Z_VB,
                                  with_lse=False, name="windowed_attention")
        oc = _diff_attention(slopes, stats, r3(lam_q1), r3(lam_k1), r3(lam_q2), r3(lam_k2), g_diff2, z,
                             pos_q_rows, pos_k_lanes, layer)
        od = _mla_attention(z)
        x = _mix_ffn(x, oa, la, ob, oc, od, mod, r3(g_post_mix), r3(g_pre_ffn), r3(g_post_ffn),
                     w_out_b, w_gu_b, w_down_b, layer)
    return x
```

```python
import functools
import math

import numpy as np
import jax
import jax.numpy as jnp
from jax import lax
from jax.experimental import pallas as pl
from jax.experimental.pallas import tpu as pltpu

F32 = jnp.float32
BF16 = jnp.bfloat16

D_MODEL = 1024
HEAD_DIM = 64
N_HEADS = 4
GROUP_W = N_HEADS * HEAD_DIM
DILATIONS = (1, 4, 16)
A_RADIUS = 64
B_RADIUS = 128
C_QK = 32
D_Q_RANK = 384
D_KV_RANK = 128
D_ROPE = 32
D_FF = 2816
ROPE_THETA = 10000.0
N_ALIBI = 12
EPS = 1e-6
NEG = -1e30
LANE = 128

RAW_D = 768
RAW_PASS = 2176
RAW_W = RAW_D + RAW_PASS
Z_QD, Z_KD, Z_VD = 0, 512, 1024
Z_QA, Z_KA, Z_VA = 1280, 1536, 1792
Z_QB, Z_VB = 2048, 2304
Z_QC, Z_KC, Z_VC = 2560, 2816, 3072
Z_KB = 3328
Z_PAD = 3456
Z_W = 3584

TM_PROJ = 512
TQ_DIFF = 256
TQ_MLA = 512
TK_DENSE = 512
N_STREAMS = 2
V_ROWS = HEAD_DIM + 16
LOG2E = 1.4426950408889634
FOLD_LO = 64.0
FOLD_RANGE = 16384
AUG_ROWS = 16
BAND_BLK = 128
BAND_UNROLL = 4
VMEM_LIMIT = 56 * 1024 * 1024


def _cparams(sem):
    return pltpu.CompilerParams(dimension_semantics=sem, vmem_limit_bytes=VMEM_LIMIT)


def _rms(x, g):
    ms = jnp.mean(x * x, axis=-1, keepdims=True)
    return x * lax.rsqrt(ms + EPS) * g


def _dot(a, b):
    return jnp.dot(a, b, preferred_element_type=F32)


def _dot_nt(a, b):
    return lax.dot_general(a, b, (((1,), (1,)), ((), ())), preferred_element_type=F32)


def _lane_iota(w):
    return lax.broadcasted_iota(jnp.int32, (1, w), 1)


def _lane_mask(w, lo, hi):
    lane = _lane_iota(w)
    return (lane >= lo) & (lane < hi)


def _ada_kernel(c_ref, w_ref, b_ref, o_ref):
    c = c_ref[...]
    c_act = c / (1.0 + jnp.exp(-c))
    o_ref[0] = jnp.dot(c_act, w_ref[0], precision=lax.Precision.HIGHEST,
                       preferred_element_type=F32) + b_ref[0]


def _ada_modulation(c, w_ada, b_ada):
    n_layers, d, six_d = w_ada.shape
    bsz = c.shape[0]
    tn = 1024
    return pl.pallas_call(
        _ada_kernel,
        grid=(n_layers, six_d // tn),
        in_specs=[pl.BlockSpec((bsz, d), lambda l, j: (0, 0)),
                  pl.BlockSpec((1, d, tn), lambda l, j: (l, 0, j)),
                  pl.BlockSpec((1, 1, tn), lambda l, j: (l, 0, j))],
        out_specs=pl.BlockSpec((1, bsz, tn), lambda l, j: (l, 0, j)),
        out_shape=jax.ShapeDtypeStruct((n_layers, bsz, six_d), F32),
        compiler_params=_cparams(("arbitrary", "arbitrary")),
        name="ada_modulation",
    )(c, w_ada, b_ada.reshape(n_layers, 1, six_d))


def _rope_kernel(p_ref, inv_ref, cos_ref, sin_ref):
    ang = p_ref[0] * inv_ref[...]
    cos_ref[0] = jnp.cos(ang)
    sin_ref[0] = jnp.sin(ang)


def _rope_tables(pos_col, inv_full):
    bsz, t, _ = pos_col.shape
    tm = 1024
    spec = pl.BlockSpec((1, tm, LANE), lambda b, i: (b, i, 0))
    return pl.pallas_call(
        _rope_kernel,
        grid=(bsz, t // tm),
        in_specs=[pl.BlockSpec((1, tm, 1), lambda b, i: (b, i, 0)),
                  pl.BlockSpec((1, LANE), lambda b, i: (0, 0))],
        out_specs=[spec, spec],
        out_shape=[jax.ShapeDtypeStruct((bsz, t, LANE), F32)] * 2,
        compiler_params=_cparams(("arbitrary", "arbitrary")),
        name="rope_tables",
    )(pos_col, inv_full)


def _inproj_kernel(x_ref, mod_ref, g_ref, w1_ref, cs_ref, gq_ref, gkv_ref, wq2_ref, wkv2_ref,
                   cos_ref, sin_ref, z_ref, zf4_ref, zf16_ref, fold_ref, *, d_scale):
    x = x_ref[0]
    tm = x.shape[0]
    mod = mod_ref[0]
    shift, scale = mod[:, 0:D_MODEL], mod[:, D_MODEL:2 * D_MODEL]
    h = (_rms(x, g_ref[...]) * (1.0 + scale) + shift).astype(BF16)
    raw = _dot(h, w1_ref[...])
    cq = raw[:, 0:D_Q_RANK]
    ckv = raw[:, D_Q_RANK:D_Q_RANK + D_KV_RANK]
    kr_a = raw[:, 512:640]
    kr_b = raw[:, 640:768]
    cos = cos_ref[0]
    sin = sin_ref[0]
    cos4 = jnp.concatenate([cos] * N_HEADS, axis=1)
    sin4 = jnp.concatenate([sin] * N_HEADS, axis=1)
    q2 = _dot(_rms(cq, gq_ref[...]).astype(BF16), wq2_ref[...])
    qd = (q2[:, 0:512] * cos4 + q2[:, 512:1024] * sin4) * d_scale
    kv2 = _dot(_rms(ckv, gkv_ref[...]).astype(BF16), wkv2_ref[...])
    kr = kr_a * cos + kr_b * sin
    kd = kv2[:, 0:512] + jnp.concatenate([kr] * N_HEADS, axis=1)
    z_ref[0, :, Z_QD:Z_QD + 512] = qd.astype(BF16)
    z_ref[0, :, Z_KD:Z_KD + 512] = kd.astype(BF16)
    z_ref[0, :, Z_VD:Z_VD + 256] = kv2[:, 512:768].astype(BF16)
    passed = raw[:, RAW_D:RAW_W] * cs_ref[...]
    z_ref[0, :, Z_QA:Z_PAD] = passed.astype(BF16)
    z_ref[0, :, Z_PAD:Z_W] = jnp.zeros((tm, Z_W - Z_PAD), BF16)
    n_slab = 3 * GROUP_W // LANE
    for s in range(n_slab):
        fold_ref[s] = passed[:, s * LANE:(s + 1) * LANE]
    for dil, out_ref in ((DILATIONS[1], zf4_ref), (DILATIONS[2], zf16_ref)):
        for r in range(dil):
            for s in range(n_slab):
                out_ref[0, r, :, s * LANE:(s + 1) * LANE] = (
                    fold_ref[s, pl.ds(r, tm // dil, stride=dil), :].astype(BF16))


def _input_projection(x, mod, g_pre, w1, col_scale, g_q, g_kv, wq2, wkv2, cos_t, sin_t, layer):
    bsz, t, d = x.shape
    tm = TM_PROJ
    const = lambda b, i: (0, 0)
    lsel = lambda b, i: (layer, 0, 0)
    kern = functools.partial(_inproj_kernel, d_scale=(HEAD_DIM + D_ROPE) ** -0.5 * LOG2E)
    return pl.pallas_call(
        kern,
        grid=(bsz, t // tm),
        in_specs=[pl.BlockSpec((1, tm, d), lambda b, i: (b, i, 0)),
                  pl.BlockSpec((None, 1, 1, 6 * d), lambda b, i: (layer, b, 0, 0)),
                  pl.BlockSpec((None, 1, d), lsel),
                  pl.BlockSpec((None, d, RAW_W), lsel),
                  pl.BlockSpec((1, RAW_PASS), const),
                  pl.BlockSpec((None, 1, D_Q_RANK), lsel),
                  pl.BlockSpec((None, 1, D_KV_RANK), lsel),
                  pl.BlockSpec((None, D_Q_RANK, 1024), lsel),
                  pl.BlockSpec((None, D_KV_RANK, 768), lsel),
                  pl.BlockSpec((1, tm, LANE), lambda b, i: (b, i, 0)),
                  pl.BlockSpec((1, tm, LANE), lambda b, i: (b, i, 0))],
        out_specs=[pl.BlockSpec((1, tm, Z_W), lambda b, i: (b, i, 0))] +
                  [pl.BlockSpec((1, dil, tm // dil, 3 * GROUP_W), lambda b, i: (b, 0, i, 0))
                   for dil in DILATIONS[1:]],
        out_shape=[jax.ShapeDtypeStruct((bsz, t, Z_W), BF16)] +
                  [jax.ShapeDtypeStruct((bsz, dil, t // dil, 3 * GROUP_W), BF16) for dil in DILATIONS[1:]],
        scratch_shapes=[pltpu.VMEM((3 * GROUP_W // LANE, tm, LANE), F32)],
        compiler_params=_cparams(("arbitrary", "arbitrary")),
        name="input_projection",
    )(x, mod, g_pre, w1, col_scale, g_q, g_kv, wq2, wkv2, cos_t, sin_t)


def _banded_kernel(sl_ref, sink_ref, q_ref, k_ref, v_ref, pk_ref, pq_ref, *rest,
                   n, dil, radius, groups, with_lse):
    if with_lse:
        o_ref, lse_ref, vm_ref = rest
    else:
        o_ref, vm_ref = rest
        lse_ref = None
    blk = BAND_BLK
    win = min(blk + 2 * radius, n)
    n_g = len(groups)
    v_all = v_ref[...]
    lane_o = _lane_iota(GROUP_W)
    for g, grp in enumerate(groups):
        vmask = (lane_o >= grp[4]) & (lane_o < grp[5])
        vm_ref[g] = jnp.where(vmask, v_all, jnp.zeros_like(v_all))
    stat_row = lax.broadcasted_iota(jnp.int32, (LANE, 1), 0)

    def one_block(i):
        r0 = pl.multiple_of(i * blk, blk)
        start = pl.multiple_of(jnp.clip(r0 - radius, 0, n - win), radius)
        q = q_ref[pl.ds(r0, blk), :].astype(F32)
        q_t = {}
        parts = []
        for (q_lo, q_w, qm_lo, qm_hi, _, _, _, _) in groups:
            if q_lo not in q_t:
                q_t[q_lo] = q[:, q_lo:q_lo + q_w].T
            dim = lax.broadcasted_iota(jnp.int32, (q_w, 1), 0)
            parts.append(jnp.where((dim >= qm_lo) & (dim < qm_hi), q_t[q_lo], 0.0))
        qs_t = jnp.concatenate(parts, axis=1).astype(BF16)
        s_t = _dot(k_ref[pl.ds(start, win), :], qs_t)
        dist = jnp.abs(pk_ref[pl.ds(start, win), :] - pq_ref[i])
        ki = start + lax.broadcasted_iota(jnp.int32, (win, 1), 0)
        qi = r0 + lax.broadcasted_iota(jnp.int32, (1, blk), 1)
        valid = jnp.abs(qi - ki) <= radius
        p_parts = []
        inv_rows = jnp.zeros((LANE, blk), F32)
        lse_rows = jnp.zeros((LANE, blk), F32)
        for g, grp in enumerate(groups):
            sg = s_t[:, g * blk:(g + 1) * blk] - (sl_ref[grp[6]] * LOG2E) * dist
            sg = jnp.where(valid, sg, NEG)
            m = jnp.max(sg, axis=0, keepdims=True)
            if grp[7] is not None:
                sk = sink_ref[grp[7]] * LOG2E
                m = jnp.maximum(m, sk)
            p = jnp.exp2(sg - m)
            den = jnp.sum(p, axis=0, keepdims=True)
            if grp[7] is not None:
                den = den + jnp.exp2(sk - m)
            p_parts.append(p.T.astype(BF16))
            inv_rows = jnp.where(stat_row == g, 1.0 / den, inv_rows)
            lse_rows = jnp.where(stat_row == g, m + jnp.log2(den), lse_rows)
        pcat = jnp.concatenate(p_parts, axis=1)
        vcat = jnp.concatenate([vm_ref[g, pl.ds(start, win), :] for g in range(n_g)], axis=0)
        o = _dot(pcat, vcat)
        inv_cols = inv_rows.T
        lse_cols = lse_rows.T
        inv_full = jnp.zeros((blk, GROUP_W), F32)
        lse_full = jnp.zeros((blk, GROUP_W), F32)
        for g, grp in enumerate(groups):
            vmask = (lane_o >= grp[4]) & (lane_o < grp[5])
            inv_full = jnp.where(vmask, inv_cols[:, g:g + 1], inv_full)
            lse_full = jnp.where(vmask, lse_cols[:, g:g + 1], lse_full)
        o = o * inv_full
        if not with_lse:
            o_ref[pl.ds(r0, blk), :] = o.astype(o_ref.dtype)
            return
        if dil == 1:
            rows = pl.ds(r0, blk)
        else:
            rows = pl.ds(r0 * dil + pl.program_id(1), blk, stride=dil)
        for s in range(GROUP_W // LANE):
            o_ref[s, rows, :] = o[:, s * LANE:(s + 1) * LANE]
            lse_ref[s, rows, :] = lse_full[:, s * LANE:(s + 1) * LANE]

    unroll = min(BAND_UNROLL, n // blk)

    def body(j, carry):
        for u in range(unroll):
            one_block(j * unroll + u)
        return carry

    lax.fori_loop(0, n // (blk * unroll), body, 0)


def _banded_attention(slopes, sink, src, pos, *, dil, radius, groups, q_col, q_w, k_col, k_w, v_col,
                      with_lse, name):
    bsz, t = pos.shape
    n = t // dil
    blk = BAND_BLK
    pos_f = pos.reshape(bsz, n, dil).transpose(0, 2, 1)
    pos_k = jnp.broadcast_to(pos_f[..., None], (bsz, dil, n, LANE))
    pos_q = pos_f.reshape(bsz, dil, n // blk, 1, blk)
    kern = functools.partial(_banded_kernel, n=n, dil=dil, radius=radius, groups=groups, with_lse=with_lse)
    if with_lse:
        o_spec = pl.BlockSpec((None, GROUP_W // LANE, t, LANE), lambda b, r: (b, 0, 0, 0))
        out_specs = [o_spec, o_spec]
        out_shape = [jax.ShapeDtypeStruct((bsz, GROUP_W // LANE, t, LANE), F32)] * 2
    else:
        out_specs = [pl.BlockSpec((None, t, GROUP_W), lambda b, r: (b, 0, 0))]
        out_shape = [jax.ShapeDtypeStruct((bsz, t, GROUP_W), BF16)]
    smem = pl.BlockSpec(memory_space=pltpu.SMEM)

    def zspec(col, w):
        if dil == 1:
            return pl.BlockSpec((None, n, w), lambda b, r: (b, 0, col // w))
        return pl.BlockSpec((None, None, n, w), lambda b, r: (b, r, 0, col // w))

    return pl.pallas_call(
        kern,
        grid=(bsz, dil),
        in_specs=[smem, smem, zspec(q_col, q_w), zspec(k_col, k_w), zspec(v_col, GROUP_W),
                  pl.BlockSpec((None, None, n, LANE), lambda b, r: (b, r, 0, 0)),
                  pl.BlockSpec((None, None, n // blk, 1, blk), lambda b, r: (b, r, 0, 0, 0))],
        out_specs=out_specs,
        out_shape=out_shape,
        scratch_shapes=[pltpu.VMEM((len(groups), n, GROUP_W), BF16)],
        compiler_params=_cparams(("arbitrary", "arbitrary")),
        name=name,
    )(slopes, sink, src, src, src, pos_k, pos_q)


def _cat_lanes(xs):
    return xs[0] if len(xs) == 1 else jnp.concatenate(xs, axis=1)


def _flash_loop(n_kb, streams, vs_fn, n_maps, tq, tk):
    n = n_maps * tq
    n_g = 2 * n_maps
    hd = HEAD_DIM
    pv_w = 2 * LANE
    n_pv = 2 * (n // pv_w)

    def scores_part(tile_fn, s_ref, g):
        s = tile_fn()
        s_ref[:, g * tq:(g + 1) * tq] = s
        return jnp.max(s, axis=0, keepdims=True)

    def softmax_part(s_ref, p_ref, m, col_max, g):
        mn = jnp.maximum(m[g], col_max[g])
        h, mp = divmod(g, n_maps)
        p_ref[h * tk:(h + 1) * tk, mp * tq:(mp + 1) * tq] = (
            jnp.exp2(s_ref[:, g * tq:(g + 1) * tq] - mn).astype(BF16))
        return mn, jnp.exp2(m[g] - mn)

    def pv_part(kb, p_ref, acc_ref, alphas, piece):
        h, j = divmod(piece, n // pv_w)
        c = slice(j * pv_w, (j + 1) * pv_w)
        pv = _dot(vs_fn(kb, h), p_ref[h * tk:(h + 1) * tk, c])
        acc_ref[h, :, c] = alphas[h][:, c] * acc_ref[h, :, c] + pv

    def pack_alphas(alphas):
        return _cat_lanes(alphas[0:n_maps]), _cat_lanes(alphas[n_maps:n_g])

    def scores_stage(kb, stream):
        scores_fn, fix_fn, s_refs, _, _ = stream
        tiles = scores_fn(kb)
        col_max = [scores_part(tiles[g], s_refs[0], g) for g in range(n_g)]
        return col_max if fix_fn is None else fix_fn(kb, s_refs[0], col_max)

    def step(kb, cur, state):
        nxt = 1 - cur
        kb_next = jnp.minimum(kb + 1, n_kb - 1)
        kb_prev = jnp.maximum(kb - 1, 0)
        tiles = [stream[0](kb_next) for stream in streams]
        new = [([], [], []) for _ in streams]
        for g in range(n_g):
            for (_, _, s_refs, p_refs, acc_ref), (m, col_max, a_prev), (m_new, alphas, col_max_next), tl in zip(
                    streams, state, new, tiles):
                mn, alpha = softmax_part(s_refs[cur], p_refs[cur], m, col_max, g)
                m_new.append(mn)
                alphas.append(alpha)
                col_max_next.append(scores_part(tl[g], s_refs[nxt], g))
                for piece in range(g * n_pv // n_g, (g + 1) * n_pv // n_g):
                    pv_part(kb_prev, p_refs[nxt], acc_ref, a_prev, piece)
        out = []
        for (_, fix_fn, s_refs, _, _), (m_new, alphas, col_max_next) in zip(streams, new):
            if fix_fn is not None:
                col_max_next = fix_fn(kb_next, s_refs[nxt], col_max_next)
            out.append((m_new, col_max_next, pack_alphas(alphas)))
        return out

    def body(i, state):
        return step(2 * i + 1, 1, step(2 * i, 0, state))

    init = []
    for stream in streams:
        stream[4][...] = jnp.zeros((2, V_ROWS, n), F32)
        stream[3][1][...] = jnp.zeros((2 * tk, n), BF16)
        init.append(([jnp.full((1, tq), NEG, F32)] * n_g, scores_stage(0, stream),
                     (jnp.ones((1, n), F32), jnp.ones((1, n), F32))))
    state = lax.fori_loop(0, n_kb // 2, body, init)
    results = []
    for (_, _, _, p_refs, acc_ref), (_, _, a_last) in zip(streams, state):
        for piece in range(n_pv):
            pv_part(n_kb - 1, p_refs[1], acc_ref, a_last, piece)
        results.append((acc_ref[0, 0:hd], acc_ref[1, 0:hd], acc_ref[0, hd:hd + 1], acc_ref[1, hd:hd + 1]))
    return results


def _flash_scratch(n_maps, tq, tk):
    n = n_maps * tq
    return ([pltpu.VMEM((tk, 2 * n), F32)] * 2 + [pltpu.VMEM((2 * tk, n), BF16)] * 2
            + [pltpu.VMEM((2, V_ROWS, n), F32)])


def _fill_vt(v_ref, vt_ref, t, tk):
    v_t = v_ref[0].astype(F32).T
    row = lax.broadcasted_iota(jnp.int32, (V_ROWS - HEAD_DIM, tk), 0)
    ones = jnp.where(row == 0, 1.0, 0.0)
    for kb in range(t // tk):
        for h in range(2):
            vals = v_t[h * HEAD_DIM:(h + 1) * HEAD_DIM, kb * tk:(kb + 1) * tk]
            vt_ref[h, kb] = jnp.concatenate([vals, ones], axis=0).astype(BF16)


def _split3(x):
    x1 = x.astype(BF16).astype(F32)
    x2 = (x - x1).astype(BF16).astype(F32)
    x3 = (x - x1 - x2).astype(BF16).astype(F32)
    return x1, x2, x3


def _diff_kernel(sl_ref, base_ref, qlo_ref, qhi_ref, klo_ref, khi_ref, lq1_ref, lk1_ref, lq2_ref, lk2_ref,
                 gd_ref, q_ref, k_ref, v_ref, pq_ref, pk_ref, o_ref, vt_ref, kx_ref, *flash_scratch,
                 t, lam_init):
    tq, tk = TQ_DIFF, TK_DENSE
    n_q, n_k = t // tq, t // tk
    b = pl.program_id(0)
    hp = pl.program_id(1)
    qi = pl.program_id(2)
    base = base_ref[b]
    base_f = base.astype(F32)
    lane = _lane_iota(LANE)

    @pl.when(qi == 0)
    def _():
        _fill_vt(v_ref, vt_ref, t, tk)
        p_rel = pk_ref[0] - base_f
        p_hi = jnp.floor(p_rel * (1.0 / FOLD_LO)) * FOLD_LO
        p_lo = p_rel - p_hi
        aug = jnp.where(lane < 3, p_hi, jnp.where(lane < 6, p_lo, jnp.where(lane < 9, 1.0, 0.0)))
        kx_ref[:, 0:LANE] = k_ref[0]
        kx_ref[:, LANE:2 * LANE] = aug.astype(BF16)

    slopes2 = [sl_ref[N_HEADS + 2 * hp] * LOG2E, sl_ref[N_HEADS + 2 * hp + 1] * LOG2E]
    row = lax.broadcasted_iota(jnp.int32, (LANE, 1), 0)
    arow = lax.broadcasted_iota(jnp.int32, (AUG_ROWS, 1), 0)
    pad_rows = jnp.zeros((LANE - AUG_ROWS, 4 * tq), BF16)

    def make_stream(st, scratch):
        q_t = q_ref[0, st * tq:(st + 1) * tq, :].astype(F32).T
        qs_t = jnp.concatenate(
            [jnp.where((row >= C_QK * g) & (row < C_QK * (g + 1)), q_t, 0.0) for g in range(4)],
            axis=1).astype(BF16)
        pq = pq_ref[0, st]
        aug_parts = []
        for h in range(2):
            m1, m2, m3 = _split3(jnp.full((1, tq), slopes2[h], F32))
            w1, w2, w3 = _split3(-slopes2[h] * (pq - base_f))
            blk = jnp.where((arow == 0) | (arow == 3), m1,
                  jnp.where((arow == 1) | (arow == 4), m2,
                  jnp.where((arow == 2) | (arow == 5), m3,
                  jnp.where(arow == 6, w1, jnp.where(arow == 7, w2, jnp.where(arow == 8, w3, 0.0))))))
            aug_parts += [blk, blk]
        aug_plus = jnp.concatenate(aug_parts, axis=1)
        q_idx = b * n_q + N_STREAMS * qi + st
        q_hi = qhi_ref[q_idx]
        q_lo = qlo_ref[q_idx]
        q_ok = q_hi - base < FOLD_RANGE

        def sign_of(kb):
            k_lo = klo_ref[b * n_k + kb]
            k_hi = khi_ref[b * n_k + kb]
            ok = q_ok & (k_hi - base < FOLD_RANGE)
            return jnp.where(ok & (k_hi <= q_lo), 1, jnp.where(ok & (k_lo >= q_hi), -1, 0))

        def scores(kb):
            off = pl.multiple_of(kb * tk, tk)
            kk = kx_ref[pl.ds(off, tk), :]
            sg = sign_of(kb).astype(F32)
            rhs = jnp.concatenate([qs_t, (sg * aug_plus).astype(BF16), pad_rows], axis=0)
            return [functools.partial(lambda g: _dot(kk, rhs[:, g * tq:(g + 1) * tq]), g)
                    for g in range(4)]

        def fix(kb, s_ref, col_max):
            def explicit():
                off = pl.multiple_of(kb * tk, tk)
                pk = pk_ref[0, pl.ds(off, tk), :]
                dist = jnp.abs(jnp.concatenate([pk] * (tq // LANE), axis=1) - pq)
                out = []
                for g in range(4):
                    s = s_ref[:, g * tq:(g + 1) * tq] - slopes2[g // 2] * dist
                    s_ref[:, g * tq:(g + 1) * tq] = s
                    out.append(jnp.max(s, axis=0, keepdims=True))
                return out
            return lax.cond(sign_of(kb) == 0, explicit, lambda: col_max)

        return (scores, fix, scratch[0:2], scratch[2:4], scratch[4])

    streams = [make_stream(st, flash_scratch[5 * st:5 * st + 5]) for st in range(N_STREAMS)]
    results = _flash_loop(n_k, streams, lambda kb, h: vt_ref[h, kb], 2, tq, tk)
    lam = (jnp.exp(jnp.sum(lq1_ref[...] * lk1_ref[...], axis=1, keepdims=True))
           - jnp.exp(jnp.sum(lq2_ref[...] * lk2_ref[...], axis=1, keepdims=True)) + lam_init)
    for st, (acc0, acc1, l0, l1) in enumerate(results):
        o0 = acc0 / l0
        o1 = acc1 / l1
        o_t = jnp.concatenate([o0[:, 0:tq] - lam * o0[:, tq:2 * tq],
                               o1[:, 0:tq] - lam * o1[:, tq:2 * tq]], axis=0)
        o = o_t.T
        o_sq = o * o
        ms0 = jnp.sum(jnp.where(lane < HEAD_DIM, o_sq, 0.0), axis=1, keepdims=True) * (1.0 / HEAD_DIM)
        ms1 = jnp.sum(jnp.where(lane >= HEAD_DIM, o_sq, 0.0), axis=1, keepdims=True) * (1.0 / HEAD_DIM)
        ms = jnp.where(lane < HEAD_DIM, ms0, ms1)
        y = o * lax.rsqrt(ms + EPS) * gd_ref[...]
        o_ref[0, st * tq:(st + 1) * tq, :] = (y * (1.0 - lam_init)).astype(o_ref.dtype)


def _block_stats(positions):
    bsz, t = positions.shape
    pq = positions.reshape(bsz, t // TQ_DIFF, TQ_DIFF)
    pk = positions.reshape(bsz, t // TK_DENSE, TK_DENSE)
    return (jnp.min(positions, axis=1), jnp.min(pq, axis=2).reshape(-1), jnp.max(pq, axis=2).reshape(-1),
            jnp.min(pk, axis=2).reshape(-1), jnp.max(pk, axis=2).reshape(-1))


def _diff_attention(slopes, stats, lq1, lk1, lq2, lk2, g_diff2, z, pos_q_rows, pos_k_lanes, layer):
    bsz, t, _ = z.shape
    tq, tk = TQ_DIFF, TK_DENSE
    lam_init = 0.8 - 0.6 * math.exp(-0.3 * layer)
    kern = functools.partial(_diff_kernel, t=t, lam_init=lam_init)
    lsel = lambda b, h, i: (layer, 0, 0)
    smem = pl.BlockSpec(memory_space=pltpu.SMEM)
    return pl.pallas_call(
        kern,
        grid=(bsz, 2, t // (N_STREAMS * tq)),
        in_specs=[smem] * 6 +
                 [pl.BlockSpec((None, 1, C_QK), lsel), pl.BlockSpec((None, 1, C_QK), lsel),
                  pl.BlockSpec((None, 1, C_QK), lsel), pl.BlockSpec((None, 1, C_QK), lsel),
                  pl.BlockSpec((None, 1, LANE), lsel),
                  pl.BlockSpec((1, N_STREAMS * tq, LANE), lambda b, h, i: (b, i, Z_QC // LANE + h)),
                  pl.BlockSpec((1, t, LANE), lambda b, h, i: (b, 0, Z_KC // LANE + h)),
                  pl.BlockSpec((1, t, LANE), lambda b, h, i: (b, 0, Z_VC // LANE + h)),
                  pl.BlockSpec((1, N_STREAMS, 1, tq), lambda b, h, i: (b, i, 0, 0)),
                  pl.BlockSpec((1, t, LANE), lambda b, h, i: (b, 0, 0))],
        out_specs=pl.BlockSpec((1, N_STREAMS * tq, LANE), lambda b, h, i: (b, i, h)),
        out_shape=jax.ShapeDtypeStruct((bsz, t, GROUP_W), BF16),
        scratch_shapes=[pltpu.VMEM((2, t // tk, V_ROWS, tk), BF16), pltpu.VMEM((t, 2 * LANE), BF16)]
                       + _flash_scratch(2, tq, tk) * N_STREAMS,
        compiler_params=_cparams(("arbitrary", "arbitrary", "arbitrary")),
        name="diff_attention",
    )(slopes, *stats, lq1, lk1, lq2, lk2, g_diff2, z, z, z, pos_q_rows, pos_k_lanes)


def _mla_kernel(q_ref, k_ref, v_ref, o_ref, vt_ref, *flash_scratch, t):
    tq, tk = TQ_MLA, TK_DENSE

    @pl.when(pl.program_id(2) == 0)
    def _():
        _fill_vt(v_ref, vt_ref, t, tk)

    def make_stream(st, scratch):
        q = q_ref[0, st * tq:(st + 1) * tq, :].astype(F32)
        q_t = [q[:, h * LANE:(h + 1) * LANE].T.astype(BF16) for h in range(2)]

        def scores(kb):
            off = pl.multiple_of(kb * tk, tk)
            kk = k_ref[0, pl.ds(off, tk), :]
            return [functools.partial(lambda h: _dot(kk[:, h * LANE:(h + 1) * LANE], q_t[h]), h)
                    for h in range(2)]

        return (scores, None, scratch[0:2], scratch[2:4], scratch[4])

    streams = [make_stream(st, flash_scratch[5 * st:5 * st + 5]) for st in range(N_STREAMS)]
    results = _flash_loop(t // tk, streams, lambda kb, h: vt_ref[h, kb], 1, tq, tk)
    for st, (acc0, acc1, l0, l1) in enumerate(results):
        o_t = jnp.concatenate([acc0 / l0, acc1 / l1], axis=0)
        o_ref[0, st * tq:(st + 1) * tq, :] = o_t.T.astype(o_ref.dtype)


def _mla_attention(z):
    bsz, t, _ = z.shape
    tq, tk = TQ_MLA, TK_DENSE
    kern = functools.partial(_mla_kernel, t=t)
    return pl.pallas_call(
        kern,
        grid=(bsz, 2, t // (N_STREAMS * tq)),
        in_specs=[pl.BlockSpec((1, N_STREAMS * tq, 2 * LANE), lambda b, h, i: (b, i, Z_QD // 256 + h)),
                  pl.BlockSpec((1, t, 2 * LANE), lambda b, h, i: (b, 0, Z_KD // 256 + h)),
                  pl.BlockSpec((1, t, LANE), lambda b, h, i: (b, 0, Z_VD // LANE + h))],
        out_specs=pl.BlockSpec((1, N_STREAMS * tq, LANE), lambda b, h, i: (b, i, h)),
        out_shape=jax.ShapeDtypeStruct((bsz, t, GROUP_W), BF16),
        scratch_shapes=[pltpu.VMEM((2, t // tk, V_ROWS, tk), BF16)] + _flash_scratch(1, tq, tk) * N_STREAMS,
        compiler_params=_cparams(("arbitrary", "arbitrary", "arbitrary")),
        name="mla_attention",
    )(z, z, z)


def _mix_ffn_kernel(x_ref, oa1_ref, oa2_ref, oa3_ref, la1_ref, la2_ref, la3_ref, ob_ref, oc_ref, od_ref,
                    mod_ref, gpm_ref, gpf_ref, gqf_ref, wout_ref, wgu_ref, wd_ref, out_ref, a_ref):
    d = D_MODEL
    x = x_ref[0]
    mod = mod_ref[0]
    gate_m = mod[:, 2 * d:3 * d]
    shift_f, scale_f, gate_f = mod[:, 3 * d:4 * d], mod[:, 4 * d:5 * d], mod[:, 5 * d:6 * d]
    slabs = lambda ref: jnp.concatenate([ref[0, s] for s in range(GROUP_W // LANE)], axis=1)
    l1, l2, l3 = slabs(la1_ref), slabs(la2_ref), slabs(la3_ref)
    mx = jnp.maximum(jnp.maximum(l1, l2), l3)
    e1, e2, e3 = jnp.exp2(l1 - mx), jnp.exp2(l2 - mx), jnp.exp2(l3 - mx)
    mix_a = (e1 * slabs(oa1_ref) + e2 * slabs(oa2_ref) + e3 * slabs(oa3_ref)) / (e1 + e2 + e3)
    mix = jnp.concatenate([mix_a.astype(BF16), ob_ref[0], oc_ref[0], od_ref[0]], axis=1)
    y = _dot(mix, wout_ref[...])
    x1 = x + gate_m * _rms(y, gpm_ref[...])
    h = (_rms(x1, gpf_ref[...]) * (1.0 + scale_f) + shift_f).astype(BF16)
    ch = 256
    for c0 in range(0, D_FF, ch):
        g = _dot(h, wgu_ref[:, c0:c0 + ch])
        u = _dot(h, wgu_ref[:, D_FF + c0:D_FF + c0 + ch])
        a_ref[:, c0:c0 + ch] = (g / (1.0 + jnp.exp(-g)) * u).astype(BF16)
    y2 = _dot(a_ref[...], wd_ref[...])
    out_ref[0] = x1 + gate_f * _rms(y2, gqf_ref[...])


def _mix_ffn(x, oa, la, ob, oc, od, mod, g_post_mix, g_pre_ffn, g_post_ffn, w_out, w_gu, w_down, layer):
    bsz, t, d = x.shape
    tm = TM_PROJ
    tok = lambda w: pl.BlockSpec((1, tm, w), lambda b, i: (b, i, 0))
    lsel = lambda b, i: (layer, 0, 0)
    once = dict(pipeline_mode=pl.Buffered(1))
    return pl.pallas_call(
        _mix_ffn_kernel,
        grid=(bsz, t // tm),
        in_specs=[tok(d)] +
                 [pl.BlockSpec((1, GROUP_W // LANE, tm, LANE), lambda b, i: (b, 0, i, 0))] * 6 +
                 [tok(GROUP_W)] * 3 +
                 [pl.BlockSpec((None, 1, 1, 6 * d), lambda b, i: (layer, b, 0, 0)),
                  pl.BlockSpec((None, 1, d), lsel), pl.BlockSpec((None, 1, d), lsel),
                  pl.BlockSpec((None, 1, d), lsel),
                  pl.BlockSpec((None, d, d), lsel, **once),
                  pl.BlockSpec((None, d, 2 * D_FF), lsel, **once),
                  pl.BlockSpec((None, D_FF, d), lsel, **once)],
        out_specs=tok(d),
        out_shape=jax.ShapeDtypeStruct((bsz, t, d), F32),
        scratch_shapes=[pltpu.VMEM((tm, D_FF), BF16)],
        compiler_params=_cparams(("arbitrary", "arbitrary")),
        name="mix_ffn",
    )(x, *oa, *la, ob, oc, od, mod, g_post_mix, g_pre_ffn, g_post_ffn, w_out, w_gu, w_down)


def _rot_cols(w):
    half = w.shape[-1] // 2
    return jnp.concatenate([-w[..., half:], w[..., :half]], axis=-1)


def _prep_w_in(w_in):
    n_layers, d, _ = w_in.shape
    z64 = jnp.zeros((n_layers, d, 64), w_in.dtype)
    z32 = jnp.zeros((n_layers, d, 32), w_in.dtype)
    col = lambda a, b: w_in[:, :, a:b]
    kr = col(2560, 2592)
    qb = 768
    vb = 1152
    parts = [col(2048, 2432), col(2432, 2560),
             z64, kr, z32,
             z64, _rot_cols(kr), z32,
             col(0, 768),
             col(qb, qb + 64), col(qb + 128, qb + 192), col(qb + 64, qb + 128), col(qb + 192, qb + 256),
             col(vb, vb + 64), col(vb, vb + 64), col(vb + 64, vb + 128), col(vb + 64, vb + 128),
             col(1280, 2048),
             col(1024, 1152)]
    return jnp.concatenate(parts, axis=-1).astype(BF16)


def _prep_w_uq(w_uq):
    n_layers, r, _ = w_uq.shape
    z64 = jnp.zeros((n_layers, r, 64), w_uq.dtype)
    z32 = jnp.zeros((n_layers, r, 32), w_uq.dtype)
    main, rot = [], []
    for h in range(N_HEADS):
        base = h * (HEAD_DIM + D_ROPE)
        nope = w_uq[:, :, base:base + HEAD_DIM]
        rope = w_uq[:, :, base + HEAD_DIM:base + HEAD_DIM + D_ROPE]
        main += [nope, rope, z32]
        rot += [z64, _rot_cols(rope), z32]
    return jnp.concatenate(main + rot, axis=-1).astype(BF16)


def _prep_w_ukv(w_ukv):
    n_layers, r, _ = w_ukv.shape
    z64 = jnp.zeros((n_layers, r, 64), w_ukv.dtype)
    ks, vs = [], []
    for h in range(N_HEADS):
        base = h * 2 * HEAD_DIM
        ks += [w_ukv[:, :, base:base + HEAD_DIM], z64]
        vs.append(w_ukv[:, :, base + HEAD_DIM:base + 2 * HEAD_DIM])
    return jnp.concatenate(ks + vs, axis=-1).astype(BF16)


def _col_scale():
    cs = np.ones((1, RAW_PASS), np.float32)
    cs[0, Z_QA - Z_QA:Z_QA - Z_QA + 256] = HEAD_DIM ** -0.5 * LOG2E
    cs[0, Z_QB - Z_QA:Z_QB - Z_QA + 256] = HEAD_DIM ** -0.5 * LOG2E
    cs[0, Z_QC - Z_QA:Z_QC - Z_QA + 256] = C_QK ** -0.5 * LOG2E
    return jnp.asarray(cs)


_A_GROUPS = tuple((0, GROUP_W, 64 * h, 64 * h + 64, 64 * h, 64 * h + 64, 8 + h, None) for h in range(4))
_B_GROUPS = tuple((128 * r, 128, 64 * g, 64 * g + 64, 128 * g + 64 * r, 128 * g + 64 * r + 64, 2 * g + r, 2 * g + r)
                  for g in range(2) for r in range(2))


def kernel(x, c, positions, w_ada, b_ada, g_pre_mix, g_post_mix, w_in, sink_logits, lam_q1, lam_k1, lam_q2, lam_k2, g_diff, g_mla_q, g_mla_kv, w_uq, w_ukv, w_out, g_pre_ffn, g_post_ffn, w_gate_up, w_down):
    bsz, t, d = x.shape
    n_layers = w_in.shape[0]
    pos = positions.astype(F32)
    pos_col = pos.reshape(bsz, t, 1)
    pos_q_rows = pos.reshape(bsz, t // TQ_DIFF, 1, TQ_DIFF)
    pos_k_lanes = jnp.broadcast_to(pos[:, :, None], (bsz, t, LANE))
    stats = _block_stats(positions)
    j = jnp.arange(1, N_ALIBI + 1, dtype=F32)
    slopes = jnp.exp2(-8.0 * j / N_ALIBI)
    half = D_ROPE // 2
    inv = jnp.power(ROPE_THETA, -jnp.arange(half, dtype=F32) / half)
    inv_full = jnp.concatenate([jnp.zeros((64,), F32), inv, inv, jnp.zeros((32,), F32)]).reshape(1, LANE)

    w1 = _prep_w_in(w_in)
    wq2 = _prep_w_uq(w_uq)
    wkv2 = _prep_w_ukv(w_ukv)
    w_out_b = w_out.astype(BF16)
    w_gu_b = w_gate_up.astype(BF16)
    w_down_b = w_down.astype(BF16)
    col_scale = _col_scale()
    r3 = lambda a: a.reshape(n_layers, 1, a.shape[-1])
    g_diff2 = r3(jnp.concatenate([g_diff, g_diff], axis=-1))
    no_sink = jnp.zeros((1,), F32)

    mod = _ada_modulation(c, w_ada, b_ada).reshape(n_layers, bsz, 1, 6 * d)
    cos_t, sin_t = _rope_tables(pos_col, inv_full)

    for layer in range(n_layers):
        z, zf4, zf16 = _input_projection(x, mod, r3(g_pre_mix), w1, col_scale, r3(g_mla_q), r3(g_mla_kv),
                                         wq2, wkv2, cos_t, sin_t, layer)
        oa, la = [], []
        for dil, src, col0 in zip(DILATIONS, (z, zf4, zf16), (Z_QA, 0, 0)):
            o_i, l_i = _banded_attention(slopes, no_sink, src, pos, dil=dil, radius=A_RADIUS, groups=_A_GROUPS,
                                         q_col=col0, q_w=GROUP_W, k_col=col0 + GROUP_W, k_w=GROUP_W,
                                         v_col=col0 + 2 * GROUP_W,
                                         with_lse=True, name="dilated_attention_%d" % dil)
            oa.append(o_i)
            la.append(l_i)
        (ob,) = _banded_attention(slopes, sink_logits[layer], z, pos, dil=1, radius=B_RADIUS, groups=_B_GROUPS,
                                  q_col=Z_QB, q_w=GROUP_W, k_col=Z_KB, k_w=LANE, v_col=Z_VB,
                                  with_lse=False, name="windowed_attention")
        oc = _diff_attention(slopes, stats, r3(lam_q1), r3(lam_k1), r3(lam_q2), r3(lam_k2), g_diff2, z,
                             pos_q_rows, pos_k_lanes, layer)
        od = _mla_attention(z)
        x = _mix_ffn(x, oa, la, ob, oc, od, mod, r3(g_post_mix), r3(g_pre_ffn), r3(g_post_ffn),
                     w_out_b, w_gu_b, w_down_b, layer)
    return x
```

```python
import functools
import math

import numpy as np
import jax
import jax.numpy as jnp
from jax import lax
from jax.experimental import pallas as pl
from jax.experimental.pallas import tpu as pltpu

F32 = jnp.float32
BF16 = jnp.bfloat16

D_MODEL = 1024
HEAD_DIM = 64
N_HEADS = 4
GROUP_W = N_HEADS * HEAD_DIM
DILATIONS = (1, 4, 16)
A_RADIUS = 64
B_RADIUS = 128
C_QK = 32
D_Q_RANK = 384
D_KV_RANK = 128
D_ROPE = 32
D_FF = 2816
ROPE_THETA = 10000.0
N_ALIBI = 12
EPS = 1e-6
NEG = -1e30
LANE = 128

RAW_D = 768
RAW_PASS = 2176
RAW_W = RAW_D + RAW_PASS
Z_QD, Z_KD, Z_VD = 0, 512, 1024
Z_QA, Z_KA, Z_VA = 1280, 1536, 1792
Z_QB, Z_VB = 2048, 2304
Z_QC, Z_KC, Z_VC = 2560, 2816, 3072
Z_KB = 3328
Z_PAD = 3456
Z_W = 3584

TM_PROJ = 512
TQ_DIFF = 256
TQ_MLA = 512
TK_DENSE = 512
N_STREAMS = 4
V_ROWS = HEAD_DIM + 16
LOG2E = 1.4426950408889634
FOLD_LO = 64.0
FOLD_RANGE = 16384
AUG_ROWS = 16
BAND_BLK = 128
BAND_UNROLL = 4
BAND_MIN_BLOCKS = 8
VMEM_LIMIT = 56 * 1024 * 1024


def _cparams(sem):
    return pltpu.CompilerParams(dimension_semantics=sem, vmem_limit_bytes=VMEM_LIMIT)


def _rms(x, g):
    ms = jnp.mean(x * x, axis=-1, keepdims=True)
    return x * lax.rsqrt(ms + EPS) * g


def _dot(a, b):
    return jnp.dot(a, b, preferred_element_type=F32)


def _dot_nt(a, b):
    return lax.dot_general(a, b, (((1,), (1,)), ((), ())), preferred_element_type=F32)


def _lane_iota(w):
    return lax.broadcasted_iota(jnp.int32, (1, w), 1)


def _lane_mask(w, lo, hi):
    lane = _lane_iota(w)
    return (lane >= lo) & (lane < hi)


def _ada_kernel(c_ref, w_ref, b_ref, o_ref):
    c = c_ref[...]
    c_act = c / (1.0 + jnp.exp(-c))
    o_ref[0] = jnp.dot(c_act, w_ref[0], precision=lax.Precision.HIGHEST,
                       preferred_element_type=F32) + b_ref[0]


def _ada_modulation(c, w_ada, b_ada):
    n_layers, d, six_d = w_ada.shape
    bsz = c.shape[0]
    tn = 1024
    return pl.pallas_call(
        _ada_kernel,
        grid=(n_layers, six_d // tn),
        in_specs=[pl.BlockSpec((bsz, d), lambda l, j: (0, 0)),
                  pl.BlockSpec((1, d, tn), lambda l, j: (l, 0, j)),
                  pl.BlockSpec((1, 1, tn), lambda l, j: (l, 0, j))],
        out_specs=pl.BlockSpec((1, bsz, tn), lambda l, j: (l, 0, j)),
        out_shape=jax.ShapeDtypeStruct((n_layers, bsz, six_d), F32),
        compiler_params=_cparams(("arbitrary", "arbitrary")),
        name="ada_modulation",
    )(c, w_ada, b_ada.reshape(n_layers, 1, six_d))


def _rope_kernel(p_ref, inv_ref, cos_ref, sin_ref):
    ang = p_ref[0] * inv_ref[...]
    cos_ref[0] = jnp.cos(ang)
    sin_ref[0] = jnp.sin(ang)


def _rope_tables(pos_col, inv_full):
    bsz, t, _ = pos_col.shape
    tm = 1024
    spec = pl.BlockSpec((1, tm, LANE), lambda b, i: (b, i, 0))
    return pl.pallas_call(
        _rope_kernel,
        grid=(bsz, t // tm),
        in_specs=[pl.BlockSpec((1, tm, 1), lambda b, i: (b, i, 0)),
                  pl.BlockSpec((1, LANE), lambda b, i: (0, 0))],
        out_specs=[spec, spec],
        out_shape=[jax.ShapeDtypeStruct((bsz, t, LANE), F32)] * 2,
        compiler_params=_cparams(("arbitrary", "arbitrary")),
        name="rope_tables",
    )(pos_col, inv_full)


def _inproj_kernel(x_ref, mod_ref, g_ref, w1_ref, cs_ref, gq_ref, gkv_ref, wq2_ref, wkv2_ref,
                   cos_ref, sin_ref, z_ref, zf4_ref, zf16_ref, fold_ref, *, d_scale):
    x = x_ref[0]
    tm = x.shape[0]
    mod = mod_ref[0]
    shift, scale = mod[:, 0:D_MODEL], mod[:, D_MODEL:2 * D_MODEL]
    h = (_rms(x, g_ref[...]) * (1.0 + scale) + shift).astype(BF16)
    raw = _dot(h, w1_ref[...])
    cq = raw[:, 0:D_Q_RANK]
    ckv = raw[:, D_Q_RANK:D_Q_RANK + D_KV_RANK]
    kr_a = raw[:, 512:640]
    kr_b = raw[:, 640:768]
    cos = cos_ref[0]
    sin = sin_ref[0]
    cos4 = jnp.concatenate([cos] * N_HEADS, axis=1)
    sin4 = jnp.concatenate([sin] * N_HEADS, axis=1)
    q2 = _dot(_rms(cq, gq_ref[...]).astype(BF16), wq2_ref[...])
    qd = (q2[:, 0:512] * cos4 + q2[:, 512:1024] * sin4) * d_scale
    kv2 = _dot(_rms(ckv, gkv_ref[...]).astype(BF16), wkv2_ref[...])
    kr = kr_a * cos + kr_b * sin
    kd = kv2[:, 0:512] + jnp.concatenate([kr] * N_HEADS, axis=1)
    z_ref[0, :, Z_QD:Z_QD + 512] = qd.astype(BF16)
    z_ref[0, :, Z_KD:Z_KD + 512] = kd.astype(BF16)
    z_ref[0, :, Z_VD:Z_VD + 256] = kv2[:, 512:768].astype(BF16)
    passed = raw[:, RAW_D:RAW_W] * cs_ref[...]
    z_ref[0, :, Z_QA:Z_PAD] = passed.astype(BF16)
    z_ref[0, :, Z_PAD:Z_W] = jnp.zeros((tm, Z_W - Z_PAD), BF16)
    n_slab = 3 * GROUP_W // LANE
    for s in range(n_slab):
        fold_ref[s] = passed[:, s * LANE:(s + 1) * LANE]
    for dil, out_ref in ((DILATIONS[1], zf4_ref), (DILATIONS[2], zf16_ref)):
        for r in range(dil):
            for s in range(n_slab):
                out_ref[0, r, :, s * LANE:(s + 1) * LANE] = (
                    fold_ref[s, pl.ds(r, tm // dil, stride=dil), :].astype(BF16))


def _input_projection(x, mod, g_pre, w1, col_scale, g_q, g_kv, wq2, wkv2, cos_t, sin_t, layer):
    bsz, t, d = x.shape
    tm = TM_PROJ
    const = lambda b, i: (0, 0)
    lsel = lambda b, i: (layer, 0, 0)
    kern = functools.partial(_inproj_kernel, d_scale=(HEAD_DIM + D_ROPE) ** -0.5 * LOG2E)
    return pl.pallas_call(
        kern,
        grid=(bsz, t // tm),
        in_specs=[pl.BlockSpec((1, tm, d), lambda b, i: (b, i, 0)),
                  pl.BlockSpec((None, 1, 1, 6 * d), lambda b, i: (layer, b, 0, 0)),
                  pl.BlockSpec((None, 1, d), lsel),
                  pl.BlockSpec((None, d, RAW_W), lsel),
                  pl.BlockSpec((1, RAW_PASS), const),
                  pl.BlockSpec((None, 1, D_Q_RANK), lsel),
                  pl.BlockSpec((None, 1, D_KV_RANK), lsel),
                  pl.BlockSpec((None, D_Q_RANK, 1024), lsel),
                  pl.BlockSpec((None, D_KV_RANK, 768), lsel),
                  pl.BlockSpec((1, tm, LANE), lambda b, i: (b, i, 0)),
                  pl.BlockSpec((1, tm, LANE), lambda b, i: (b, i, 0))],
        out_specs=[pl.BlockSpec((1, tm, Z_W), lambda b, i: (b, i, 0))] +
                  [pl.BlockSpec((1, dil, tm // dil, 3 * GROUP_W), lambda b, i: (b, 0, i, 0))
                   for dil in DILATIONS[1:]],
        out_shape=[jax.ShapeDtypeStruct((bsz, t, Z_W), BF16)] +
                  [jax.ShapeDtypeStruct((bsz, dil, t // dil, 3 * GROUP_W), BF16) for dil in DILATIONS[1:]],
        scratch_shapes=[pltpu.VMEM((3 * GROUP_W // LANE, tm, LANE), F32)],
        compiler_params=_cparams(("arbitrary", "arbitrary")),
        name="input_projection",
    )(x, mod, g_pre, w1, col_scale, g_q, g_kv, wq2, wkv2, cos_t, sin_t)


def _banded_kernel(sl_ref, sink_ref, q_ref, k_ref, v_ref, pk_ref, pq_ref, *rest,
                   n, dil, radius, groups, with_lse):
    if with_lse:
        o_ref, lse_ref, vm_ref = rest
    else:
        o_ref, vm_ref = rest
        lse_ref = None
    blk = BAND_BLK
    win = min(blk + 2 * radius, n)
    n_g = len(groups)
    n_res = q_ref.shape[0]
    lane_o = _lane_iota(GROUP_W)
    for rl in range(n_res):
        v_all = v_ref[rl]
        for g, grp in enumerate(groups):
            vmask = (lane_o >= grp[4]) & (lane_o < grp[5])
            vm_ref[rl, g] = jnp.where(vmask, v_all, jnp.zeros_like(v_all))
    stat_row = lax.broadcasted_iota(jnp.int32, (LANE, 1), 0)

    def one_block(rl, i):
        r0 = pl.multiple_of(i * blk, blk)
        start = pl.multiple_of(jnp.clip(r0 - radius, 0, n - win), radius)
        q = q_ref[rl, pl.ds(r0, blk), :].astype(F32)
        q_t = {}
        parts = []
        for (q_lo, q_w, qm_lo, qm_hi, _, _, _, _) in groups:
            if q_lo not in q_t:
                q_t[q_lo] = q[:, q_lo:q_lo + q_w].T
            dim = lax.broadcasted_iota(jnp.int32, (q_w, 1), 0)
            parts.append(jnp.where((dim >= qm_lo) & (dim < qm_hi), q_t[q_lo], 0.0))
        qs_t = jnp.concatenate(parts, axis=1).astype(BF16)
        s_t = _dot(k_ref[rl, pl.ds(start, win), :], qs_t)
        dist = jnp.abs(pk_ref[rl, pl.ds(start, win), :] - pq_ref[rl, i])
        ki = start + lax.broadcasted_iota(jnp.int32, (win, 1), 0)
        qi = r0 + lax.broadcasted_iota(jnp.int32, (1, blk), 1)
        valid = jnp.abs(qi - ki) <= radius
        p_parts = []
        inv_rows = jnp.zeros((LANE, blk), F32)
        lse_rows = jnp.zeros((LANE, blk), F32)
        for g, grp in enumerate(groups):
            sg = s_t[:, g * blk:(g + 1) * blk] - (sl_ref[grp[6]] * LOG2E) * dist
            sg = jnp.where(valid, sg, NEG)
            m = jnp.max(sg, axis=0, keepdims=True)
            if grp[7] is not None:
                sk = sink_ref[grp[7]] * LOG2E
                m = jnp.maximum(m, sk)
            p = jnp.exp2(sg - m)
            den = jnp.sum(p, axis=0, keepdims=True)
            if grp[7] is not None:
                den = den + jnp.exp2(sk - m)
            p_parts.append(p.T.astype(BF16))
            inv_rows = jnp.where(stat_row == g, 1.0 / den, inv_rows)
            lse_rows = jnp.where(stat_row == g, m + jnp.log2(den), lse_rows)
        pcat = jnp.concatenate(p_parts, axis=1)
        vcat = jnp.concatenate([vm_ref[rl, g, pl.ds(start, win), :] for g in range(n_g)], axis=0)
        o = _dot(pcat, vcat)
        inv_cols = inv_rows.T
        lse_cols = lse_rows.T
        inv_full = jnp.zeros((blk, GROUP_W), F32)
        lse_full = jnp.zeros((blk, GROUP_W), F32)
        for g, grp in enumerate(groups):
            vmask = (lane_o >= grp[4]) & (lane_o < grp[5])
            inv_full = jnp.where(vmask, inv_cols[:, g:g + 1], inv_full)
            lse_full = jnp.where(vmask, lse_cols[:, g:g + 1], lse_full)
        o = o * inv_full
        if not with_lse:
            o_ref[pl.ds(r0, blk), :] = o.astype(o_ref.dtype)
            return
        if dil == 1:
            rows = pl.ds(r0, blk)
        else:
            rows = pl.ds(r0 * dil + pl.program_id(1) * n_res + rl, blk, stride=dil)
        for s in range(GROUP_W // LANE):
            o_ref[s, rows, :] = o[:, s * LANE:(s + 1) * LANE]
            lse_ref[s, rows, :] = lse_full[:, s * LANE:(s + 1) * LANE]

    unroll = min(BAND_UNROLL, n // blk)
    trips = n // (blk * unroll)
    for rl in range(n_res):
        def body(j, carry, rl=rl):
            for u in range(unroll):
                one_block(rl, j * unroll + u)
            return carry

        if trips == 1:
            body(0, 0)
        else:
            lax.fori_loop(0, trips, body, 0)


def _banded_attention(slopes, sink, src, pos, *, dil, radius, groups, q_col, q_w, k_col, k_w, v_col,
                      with_lse, name):
    bsz, t = pos.shape
    n = t // dil
    blk = BAND_BLK
    n_res = max(1, min(dil, BAND_MIN_BLOCKS // (n // blk)))
    if dil == 1:
        src = src.reshape(bsz, 1, t, src.shape[-1])
    pos_f = pos.reshape(bsz, n, dil).transpose(0, 2, 1)
    pos_k = jnp.broadcast_to(pos_f[..., None], (bsz, dil, n, LANE))
    pos_q = pos_f.reshape(bsz, dil, n // blk, 1, blk)
    kern = functools.partial(_banded_kernel, n=n, dil=dil, radius=radius, groups=groups, with_lse=with_lse)
    if with_lse:
        o_spec = pl.BlockSpec((None, GROUP_W // LANE, t, LANE), lambda b, r: (b, 0, 0, 0))
        out_specs = [o_spec, o_spec]
        out_shape = [jax.ShapeDtypeStruct((bsz, GROUP_W // LANE, t, LANE), F32)] * 2
    else:
        out_specs = [pl.BlockSpec((None, t, GROUP_W), lambda b, r: (b, 0, 0))]
        out_shape = [jax.ShapeDtypeStruct((bsz, t, GROUP_W), BF16)]
    smem = pl.BlockSpec(memory_space=pltpu.SMEM)

    def zspec(col, w):
        return pl.BlockSpec((None, n_res, n, w), lambda b, r: (b, r, 0, col // w))

    return pl.pallas_call(
        kern,
        grid=(bsz, dil // n_res),
        in_specs=[smem, smem, zspec(q_col, q_w), zspec(k_col, k_w), zspec(v_col, GROUP_W),
                  pl.BlockSpec((None, n_res, n, LANE), lambda b, r: (b, r, 0, 0)),
                  pl.BlockSpec((None, n_res, n // blk, 1, blk), lambda b, r: (b, r, 0, 0, 0))],
        out_specs=out_specs,
        out_shape=out_shape,
        scratch_shapes=[pltpu.VMEM((n_res, len(groups), n, GROUP_W), BF16)],
        compiler_params=_cparams(("arbitrary", "arbitrary")),
        name=name,
    )(slopes, sink, src, src, src, pos_k, pos_q)


def _cat_lanes(xs):
    return xs[0] if len(xs) == 1 else jnp.concatenate(xs, axis=1)


def _flash_loop(n_kb, streams, vs_fn, n_maps, tq, tk):
    n = n_maps * tq
    n_g = 2 * n_maps
    hd = HEAD_DIM
    pv_w = 2 * LANE
    n_pv = 2 * (n // pv_w)

    def scores_part(tile_fn, s_ref, g):
        s = tile_fn()
        s_ref[:, g * tq:(g + 1) * tq] = s
        return jnp.max(s, axis=0, keepdims=True)

    def softmax_part(s_ref, p_ref, m, col_max, g):
        mn = jnp.maximum(m[g], col_max[g])
        h, mp = divmod(g, n_maps)
        p_ref[h * tk:(h + 1) * tk, mp * tq:(mp + 1) * tq] = (
            jnp.exp2(s_ref[:, g * tq:(g + 1) * tq] - mn).astype(BF16))
        return mn, jnp.exp2(m[g] - mn)

    def pv_part(kb, p_ref, acc_ref, alphas, piece):
        h, j = divmod(piece, n // pv_w)
        c = slice(j * pv_w, (j + 1) * pv_w)
        pv = _dot(vs_fn(kb, h), p_ref[h * tk:(h + 1) * tk, c])
        acc_ref[h, :, c] = alphas[h][:, c] * acc_ref[h, :, c] + pv

    def pack_alphas(alphas):
        return _cat_lanes(alphas[0:n_maps]), _cat_lanes(alphas[n_maps:n_g])

    def scores_stage(kb, stream):
        scores_fn, fix_fn, s_refs, _, _ = stream
        tiles = scores_fn(kb)
        col_max = [scores_part(tiles[g], s_refs[0], g) for g in range(n_g)]
        return col_max if fix_fn is None else fix_fn(kb, s_refs[0], col_max)

    def step(kb, cur, state):
        nxt = 1 - cur
        kb_next = jnp.minimum(kb + 1, n_kb - 1)
        kb_prev = jnp.maximum(kb - 1, 0)
        tiles = [stream[0](kb_next) for stream in streams]
        new = [([], [], []) for _ in streams]
        for g in range(n_g):
            for (_, _, s_refs, p_refs, acc_ref), (m, col_max, a_prev), (m_new, alphas, col_max_next), tl in zip(
                    streams, state, new, tiles):
                mn, alpha = softmax_part(s_refs[cur], p_refs[cur], m, col_max, g)
                m_new.append(mn)
                alphas.append(alpha)
                col_max_next.append(scores_part(tl[g], s_refs[nxt], g))
                for piece in range(g * n_pv // n_g, (g + 1) * n_pv // n_g):
                    pv_part(kb_prev, p_refs[nxt], acc_ref, a_prev, piece)
        out = []
        for (_, fix_fn, s_refs, _, _), (m_new, alphas, col_max_next) in zip(streams, new):
            if fix_fn is not None:
                col_max_next = fix_fn(kb_next, s_refs[nxt], col_max_next)
            out.append((m_new, col_max_next, pack_alphas(alphas)))
        return out

    def body(i, state):
        return step(2 * i + 1, 1, step(2 * i, 0, state))

    init = []
    for stream in streams:
        stream[4][...] = jnp.zeros((2, V_ROWS, n), F32)
        stream[3][1][...] = jnp.zeros((2 * tk, n), BF16)
        init.append(([jnp.full((1, tq), NEG, F32)] * n_g, scores_stage(0, stream),
                     (jnp.ones((1, n), F32), jnp.ones((1, n), F32))))
    state = lax.fori_loop(0, n_kb // 2, body, init)
    results = []
    for (_, _, _, p_refs, acc_ref), (_, _, a_last) in zip(streams, state):
        for piece in range(n_pv):
            pv_part(n_kb - 1, p_refs[1], acc_ref, a_last, piece)
        results.append((acc_ref[0, 0:hd], acc_ref[1, 0:hd], acc_ref[0, hd:hd + 1], acc_ref[1, hd:hd + 1]))
    return results


def _flash_scratch(n_maps, tq, tk):
    n = n_maps * tq
    return ([pltpu.VMEM((tk, 2 * n), F32)] * 2 + [pltpu.VMEM((2 * tk, n), BF16)] * 2
            + [pltpu.VMEM((2, V_ROWS, n), F32)])


def _fill_vt(v_ref, vt_ref, t, tk):
    v_t = v_ref[0].astype(F32).T
    row = lax.broadcasted_iota(jnp.int32, (V_ROWS - HEAD_DIM, tk), 0)
    ones = jnp.where(row == 0, 1.0, 0.0)
    for kb in range(t // tk):
        for h in range(2):
            vals = v_t[h * HEAD_DIM:(h + 1) * HEAD_DIM, kb * tk:(kb + 1) * tk]
            vt_ref[h, kb] = jnp.concatenate([vals, ones], axis=0).astype(BF16)


def _split3(x):
    x1 = x.astype(BF16).astype(F32)
    x2 = (x - x1).astype(BF16).astype(F32)
    x3 = (x - x1 - x2).astype(BF16).astype(F32)
    return x1, x2, x3


def _diff_kernel(sl_ref, base_ref, qlo_ref, qhi_ref, klo_ref, khi_ref, lq1_ref, lk1_ref, lq2_ref, lk2_ref,
                 gd_ref, q_ref, k_ref, v_ref, pq_ref, pk_ref, o_ref, vt_ref, kx_ref, *flash_scratch,
                 t, lam_init):
    tq, tk = TQ_DIFF, TK_DENSE
    n_q, n_k = t // tq, t // tk
    b = pl.program_id(0)
    hp = pl.program_id(1)
    qi = pl.program_id(2)
    base = base_ref[b]
    base_f = base.astype(F32)
    lane = _lane_iota(LANE)

    @pl.when(qi == 0)
    def _():
        _fill_vt(v_ref, vt_ref, t, tk)
        p_rel = pk_ref[0] - base_f
        p_hi = jnp.floor(p_rel * (1.0 / FOLD_LO)) * FOLD_LO
        p_lo = p_rel - p_hi
        aug = jnp.where(lane < 3, p_hi, jnp.where(lane < 6, p_lo, jnp.where(lane < 9, 1.0, 0.0)))
        kx_ref[:, 0:LANE] = k_ref[0]
        kx_ref[:, LANE:2 * LANE] = aug.astype(BF16)

    slopes2 = [sl_ref[N_HEADS + 2 * hp] * LOG2E, sl_ref[N_HEADS + 2 * hp + 1] * LOG2E]
    row = lax.broadcasted_iota(jnp.int32, (LANE, 1), 0)
    arow = lax.broadcasted_iota(jnp.int32, (AUG_ROWS, 1), 0)
    pad_rows = jnp.zeros((LANE - AUG_ROWS, 4 * tq), BF16)

    def make_stream(st, scratch):
        q_t = q_ref[0, st * tq:(st + 1) * tq, :].astype(F32).T
        qs_t = jnp.concatenate(
            [jnp.where((row >= C_QK * g) & (row < C_QK * (g + 1)), q_t, 0.0) for g in range(4)],
            axis=1).astype(BF16)
        pq = pq_ref[0, st]
        aug_parts = []
        for h in range(2):
            m1, m2, m3 = _split3(jnp.full((1, tq), slopes2[h], F32))
            w1, w2, w3 = _split3(-slopes2[h] * (pq - base_f))
            blk = jnp.where((arow == 0) | (arow == 3), m1,
                  jnp.where((arow == 1) | (arow == 4), m2,
                  jnp.where((arow == 2) | (arow == 5), m3,
                  jnp.where(arow == 6, w1, jnp.where(arow == 7, w2, jnp.where(arow == 8, w3, 0.0))))))
            aug_parts += [blk, blk]
        aug_plus = jnp.concatenate(aug_parts, axis=1)
        q_idx = b * n_q + N_STREAMS * qi + st
        q_hi = qhi_ref[q_idx]
        q_lo = qlo_ref[q_idx]
        q_ok = q_hi - base < FOLD_RANGE

        def sign_of(kb):
            k_lo = klo_ref[b * n_k + kb]
            k_hi = khi_ref[b * n_k + kb]
            ok = q_ok & (k_hi - base < FOLD_RANGE)
            return jnp.where(ok & (k_hi <= q_lo), 1, jnp.where(ok & (k_lo >= q_hi), -1, 0))

        def scores(kb):
            off = pl.multiple_of(kb * tk, tk)
            kk = kx_ref[pl.ds(off, tk), :]
            sg = sign_of(kb).astype(F32)
            rhs = jnp.concatenate([qs_t, (sg * aug_plus).astype(BF16), pad_rows], axis=0)
            return [functools.partial(lambda g: _dot(kk, rhs[:, g * tq:(g + 1) * tq]), g)
                    for g in range(4)]

        def fix(kb, s_ref, col_max):
            def explicit():
                off = pl.multiple_of(kb * tk, tk)
                pk = pk_ref[0, pl.ds(off, tk), :]
                dist = jnp.abs(jnp.concatenate([pk] * (tq // LANE), axis=1) - pq)
                out = []
                for g in range(4):
                    s = s_ref[:, g * tq:(g + 1) * tq] - slopes2[g // 2] * dist
                    s_ref[:, g * tq:(g + 1) * tq] = s
                    out.append(jnp.max(s, axis=0, keepdims=True))
                return out
            return lax.cond(sign_of(kb) == 0, explicit, lambda: col_max)

        return (scores, fix, scratch[0:2], scratch[2:4], scratch[4])

    streams = [make_stream(st, flash_scratch[5 * st:5 * st + 5]) for st in range(N_STREAMS)]
    results = _flash_loop(n_k, streams, lambda kb, h: vt_ref[h, kb], 2, tq, tk)
    lam = (jnp.exp(jnp.sum(lq1_ref[...] * lk1_ref[...], axis=1, keepdims=True))
           - jnp.exp(jnp.sum(lq2_ref[...] * lk2_ref[...], axis=1, keepdims=True)) + lam_init)
    for st, (acc0, acc1, l0, l1) in enumerate(results):
        o0 = acc0 / l0
        o1 = acc1 / l1
        o_t = jnp.concatenate([o0[:, 0:tq] - lam * o0[:, tq:2 * tq],
                               o1[:, 0:tq] - lam * o1[:, tq:2 * tq]], axis=0)
        o = o_t.T
        o_sq = o * o
        ms0 = jnp.sum(jnp.where(lane < HEAD_DIM, o_sq, 0.0), axis=1, keepdims=True) * (1.0 / HEAD_DIM)
        ms1 = jnp.sum(jnp.where(lane >= HEAD_DIM, o_sq, 0.0), axis=1, keepdims=True) * (1.0 / HEAD_DIM)
        ms = jnp.where(lane < HEAD_DIM, ms0, ms1)
        y = o * lax.rsqrt(ms + EPS) * gd_ref[...]
        o_ref[0, st * tq:(st + 1) * tq, :] = (y * (1.0 - lam_init)).astype(o_ref.dtype)


def _block_stats(positions):
    bsz, t = positions.shape
    pq = positions.reshape(bsz, t // TQ_DIFF, TQ_DIFF)
    pk = positions.reshape(bsz, t // TK_DENSE, TK_DENSE)
    return (jnp.min(positions, axis=1), jnp.min(pq, axis=2).reshape(-1), jnp.max(pq, axis=2).reshape(-1),
            jnp.min(pk, axis=2).reshape(-1), jnp.max(pk, axis=2).reshape(-1))


def _diff_attention(slopes, stats, lq1, lk1, lq2, lk2, g_diff2, z, pos_q_rows, pos_k_lanes, layer):
    bsz, t, _ = z.shape
    tq, tk = TQ_DIFF, TK_DENSE
    lam_init = 0.8 - 0.6 * math.exp(-0.3 * layer)
    kern = functools.partial(_diff_kernel, t=t, lam_init=lam_init)
    lsel = lambda b, h, i: (layer, 0, 0)
    smem = pl.BlockSpec(memory_space=pltpu.SMEM)
    return pl.pallas_call(
        kern,
        grid=(bsz, 2, t // (N_STREAMS * tq)),
        in_specs=[smem] * 6 +
                 [pl.BlockSpec((None, 1, C_QK), lsel), pl.BlockSpec((None, 1, C_QK), lsel),
                  pl.BlockSpec((None, 1, C_QK), lsel), pl.BlockSpec((None, 1, C_QK), lsel),
                  pl.BlockSpec((None, 1, LANE), lsel),
                  pl.BlockSpec((1, N_STREAMS * tq, LANE), lambda b, h, i: (b, i, Z_QC // LANE + h)),
                  pl.BlockSpec((1, t, LANE), lambda b, h, i: (b, 0, Z_KC // LANE + h)),
                  pl.BlockSpec((1, t, LANE), lambda b, h, i: (b, 0, Z_VC // LANE + h)),
                  pl.BlockSpec((1, N_STREAMS, 1, tq), lambda b, h, i: (b, i, 0, 0)),
                  pl.BlockSpec((1, t, LANE), lambda b, h, i: (b, 0, 0))],
        out_specs=pl.BlockSpec((1, N_STREAMS * tq, LANE), lambda b, h, i: (b, i, h)),
        out_shape=jax.ShapeDtypeStruct((bsz, t, GROUP_W), BF16),
        scratch_shapes=[pltpu.VMEM((2, t // tk, V_ROWS, tk), BF16), pltpu.VMEM((t, 2 * LANE), BF16)]
                       + _flash_scratch(2, tq, tk) * N_STREAMS,
        compiler_params=_cparams(("arbitrary", "arbitrary", "arbitrary")),
        name="diff_attention",
    )(slopes, *stats, lq1, lk1, lq2, lk2, g_diff2, z, z, z, pos_q_rows, pos_k_lanes)


def _mla_kernel(q_ref, k_ref, v_ref, o_ref, vt_ref, *flash_scratch, t):
    tq, tk = TQ_MLA, TK_DENSE

    @pl.when(pl.program_id(2) == 0)
    def _():
        _fill_vt(v_ref, vt_ref, t, tk)

    def make_stream(st, scratch):
        q = q_ref[0, st * tq:(st + 1) * tq, :].astype(F32)
        q_t = [q[:, h * LANE:(h + 1) * LANE].T.astype(BF16) for h in range(2)]

        def scores(kb):
            off = pl.multiple_of(kb * tk, tk)
            kk = k_ref[0, pl.ds(off, tk), :]
            return [functools.partial(lambda h: _dot(kk[:, h * LANE:(h + 1) * LANE], q_t[h]), h)
                    for h in range(2)]

        return (scores, None, scratch[0:2], scratch[2:4], scratch[4])

    streams = [make_stream(st, flash_scratch[5 * st:5 * st + 5]) for st in range(N_STREAMS)]
    results = _flash_loop(t // tk, streams, lambda kb, h: vt_ref[h, kb], 1, tq, tk)
    for st, (acc0, acc1, l0, l1) in enumerate(results):
        o_t = jnp.concatenate([acc0 / l0, acc1 / l1], axis=0)
        o_ref[0, st * tq:(st + 1) * tq, :] = o_t.T.astype(o_ref.dtype)


def _mla_attention(z):
    bsz, t, _ = z.shape
    tq, tk = TQ_MLA, TK_DENSE
    kern = functools.partial(_mla_kernel, t=t)
    return pl.pallas_call(
        kern,
        grid=(bsz, 2, t // (N_STREAMS * tq)),
        in_specs=[pl.BlockSpec((1, N_STREAMS * tq, 2 * LANE), lambda b, h, i: (b, i, Z_QD // 256 + h)),
                  pl.BlockSpec((1, t, 2 * LANE), lambda b, h, i: (b, 0, Z_KD // 256 + h)),
                  pl.BlockSpec((1, t, LANE), lambda b, h, i: (b, 0, Z_VD // LANE + h))],
        out_specs=pl.BlockSpec((1, N_STREAMS * tq, LANE), lambda b, h, i: (b, i, h)),
        out_shape=jax.ShapeDtypeStruct((bsz, t, GROUP_W), BF16),
        scratch_shapes=[pltpu.VMEM((2, t // tk, V_ROWS, tk), BF16)] + _flash_scratch(1, tq, tk) * N_STREAMS,
        compiler_params=_cparams(("arbitrary", "arbitrary", "arbitrary")),
        name="mla_attention",
    )(z, z, z)


def _mix_ffn_kernel(x_ref, oa1_ref, oa2_ref, oa3_ref, la1_ref, la2_ref, la3_ref, ob_ref, oc_ref, od_ref,
                    mod_ref, gpm_ref, gpf_ref, gqf_ref, wout_ref, wgu_ref, wd_ref, out_ref, a_ref):
    d = D_MODEL
    x = x_ref[0]
    mod = mod_ref[0]
    gate_m = mod[:, 2 * d:3 * d]
    shift_f, scale_f, gate_f = mod[:, 3 * d:4 * d], mod[:, 4 * d:5 * d], mod[:, 5 * d:6 * d]
    slabs = lambda ref: jnp.concatenate([ref[0, s] for s in range(GROUP_W // LANE)], axis=1)
    l1, l2, l3 = slabs(la1_ref), slabs(la2_ref), slabs(la3_ref)
    mx = jnp.maximum(jnp.maximum(l1, l2), l3)
    e1, e2, e3 = jnp.exp2(l1 - mx), jnp.exp2(l2 - mx), jnp.exp2(l3 - mx)
    mix_a = (e1 * slabs(oa1_ref) + e2 * slabs(oa2_ref) + e3 * slabs(oa3_ref)) / (e1 + e2 + e3)
    mix = jnp.concatenate([mix_a.astype(BF16), ob_ref[0], oc_ref[0], od_ref[0]], axis=1)
    y = _dot(mix, wout_ref[...])
    x1 = x + gate_m * _rms(y, gpm_ref[...])
    h = (_rms(x1, gpf_ref[...]) * (1.0 + scale_f) + shift_f).astype(BF16)
    ch = 256
    for c0 in range(0, D_FF, ch):
        g = _dot(h, wgu_ref[:, c0:c0 + ch])
        u = _dot(h, wgu_ref[:, D_FF + c0:D_FF + c0 + ch])
        a_ref[:, c0:c0 + ch] = (g / (1.0 + jnp.exp(-g)) * u).astype(BF16)
    y2 = _dot(a_ref[...], wd_ref[...])
    out_ref[0] = x1 + gate_f * _rms(y2, gqf_ref[...])


def _mix_ffn(x, oa, la, ob, oc, od, mod, g_post_mix, g_pre_ffn, g_post_ffn, w_out, w_gu, w_down, layer):
    bsz, t, d = x.shape
    tm = TM_PROJ
    tok = lambda w: pl.BlockSpec((1, tm, w), lambda b, i: (b, i, 0))
    lsel = lambda b, i: (layer, 0, 0)
    once = dict(pipeline_mode=pl.Buffered(1))
    return pl.pallas_call(
        _mix_ffn_kernel,
        grid=(bsz, t // tm),
        in_specs=[tok(d)] +
                 [pl.BlockSpec((1, GROUP_W // LANE, tm, LANE), lambda b, i: (b, 0, i, 0))] * 6 +
                 [tok(GROUP_W)] * 3 +
                 [pl.BlockSpec((None, 1, 1, 6 * d), lambda b, i: (layer, b, 0, 0)),
                  pl.BlockSpec((None, 1, d), lsel), pl.BlockSpec((None, 1, d), lsel),
                  pl.BlockSpec((None, 1, d), lsel),
                  pl.BlockSpec((None, d, d), lsel, **once),
                  pl.BlockSpec((None, d, 2 * D_FF), lsel, **once),
                  pl.BlockSpec((None, D_FF, d), lsel, **once)],
        out_specs=tok(d),
        out_shape=jax.ShapeDtypeStruct((bsz, t, d), F32),
        scratch_shapes=[pltpu.VMEM((tm, D_FF), BF16)],
        compiler_params=_cparams(("arbitrary", "arbitrary")),
        name="mix_ffn",
    )(x, *oa, *la, ob, oc, od, mod, g_post_mix, g_pre_ffn, g_post_ffn, w_out, w_gu, w_down)


def _rot_cols(w):
    half = w.shape[-1] // 2
    return jnp.concatenate([-w[..., half:], w[..., :half]], axis=-1)


def _prep_w_in(w_in):
    n_layers, d, _ = w_in.shape
    z64 = jnp.zeros((n_layers, d, 64), w_in.dtype)
    z32 = jnp.zeros((n_layers, d, 32), w_in.dtype)
    col = lambda a, b: w_in[:, :, a:b]
    kr = col(2560, 2592)
    qb = 768
    vb = 1152
    parts = [col(2048, 2432), col(2432, 2560),
             z64, kr, z32,
             z64, _rot_cols(kr), z32,
             col(0, 768),
             col(qb, qb + 64), col(qb + 128, qb + 192), col(qb + 64, qb + 128), col(qb + 192, qb + 256),
             col(vb, vb + 64), col(vb, vb + 64), col(vb + 64, vb + 128), col(vb + 64, vb + 128),
             col(1280, 2048),
             col(1024, 1152)]
    return jnp.concatenate(parts, axis=-1).astype(BF16)


def _prep_w_uq(w_uq):
    n_layers, r, _ = w_uq.shape
    z64 = jnp.zeros((n_layers, r, 64), w_uq.dtype)
    z32 = jnp.zeros((n_layers, r, 32), w_uq.dtype)
    main, rot = [], []
    for h in range(N_HEADS):
        base = h * (HEAD_DIM + D_ROPE)
        nope = w_uq[:, :, base:base + HEAD_DIM]
        rope = w_uq[:, :, base + HEAD_DIM:base + HEAD_DIM + D_ROPE]
        main += [nope, rope, z32]
        rot += [z64, _rot_cols(rope), z32]
    return jnp.concatenate(main + rot, axis=-1).astype(BF16)


def _prep_w_ukv(w_ukv):
    n_layers, r, _ = w_ukv.shape
    z64 = jnp.zeros((n_layers, r, 64), w_ukv.dtype)
    ks, vs = [], []
    for h in range(N_HEADS):
        base = h * 2 * HEAD_DIM
        ks += [w_ukv[:, :, base:base + HEAD_DIM], z64]
        vs.append(w_ukv[:, :, base + HEAD_DIM:base + 2 * HEAD_DIM])
    return jnp.concatenate(ks + vs, axis=-1).astype(BF16)


def _col_scale():
    cs = np.ones((1, RAW_PASS), np.float32)
    cs[0, Z_QA - Z_QA:Z_QA - Z_QA + 256] = HEAD_DIM ** -0.5 * LOG2E
    cs[0, Z_QB - Z_QA:Z_QB - Z_QA + 256] = HEAD_DIM ** -0.5 * LOG2E
    cs[0, Z_QC - Z_QA:Z_QC - Z_QA + 256] = C_QK ** -0.5 * LOG2E
    return jnp.asarray(cs)


_A_GROUPS = tuple((0, GROUP_W, 64 * h, 64 * h + 64, 64 * h, 64 * h + 64, 8 + h, None) for h in range(4))
_B_GROUPS = tuple((128 * r, 128, 64 * g, 64 * g + 64, 128 * g + 64 * r, 128 * g + 64 * r + 64, 2 * g + r, 2 * g + r)
                  for g in range(2) for r in range(2))


def kernel(x, c, positions, w_ada, b_ada, g_pre_mix, g_post_mix, w_in, sink_logits, lam_q1, lam_k1, lam_q2, lam_k2, g_diff, g_mla_q, g_mla_kv, w_uq, w_ukv, w_out, g_pre_ffn, g_post_ffn, w_gate_up, w_down):
    bsz, t, d = x.shape
    n_layers = w_in.shape[0]
    pos = positions.astype(F32)
    pos_col = pos.reshape(bsz, t, 1)
    pos_q_rows = pos.reshape(bsz, t // TQ_DIFF, 1, TQ_DIFF)
    pos_k_lanes = jnp.broadcast_to(pos[:, :, None], (bsz, t, LANE))
    stats = _block_stats(positions)
    j = jnp.arange(1, N_ALIBI + 1, dtype=F32)
    slopes = jnp.exp2(-8.0 * j / N_ALIBI)
    half = D_ROPE // 2
    inv = jnp.power(ROPE_THETA, -jnp.arange(half, dtype=F32) / half)
    inv_full = jnp.concatenate([jnp.zeros((64,), F32), inv, inv, jnp.zeros((32,), F32)]).reshape(1, LANE)

    w1 = _prep_w_in(w_in)
    wq2 = _prep_w_uq(w_uq)
    wkv2 = _prep_w_ukv(w_ukv)
    w_out_b = w_out.astype(BF16)
    w_gu_b = w_gate_up.astype(BF16)
    w_down_b = w_down.astype(BF16)
    col_scale = _col_scale()
    r3 = lambda a: a.reshape(n_layers, 1, a.shape[-1])
    g_diff2 = r3(jnp.concatenate([g_diff, g_diff], axis=-1))
    no_sink = jnp.zeros((1,), F32)

    mod = _ada_modulation(c, w_ada, b_ada).reshape(n_layers, bsz, 1, 6 * d)
    cos_t, sin_t = _rope_tables(pos_col, inv_full)

    for layer in range(n_layers):
        z, zf4, zf16 = _input_projection(x, mod, r3(g_pre_mix), w1, col_scale, r3(g_mla_q), r3(g_mla_kv),
                                         wq2, wkv2, cos_t, sin_t, layer)
        oa, la = [], []
        for dil, src, col0 in zip(DILATIONS, (z, zf4, zf16), (Z_QA, 0, 0)):
            o_i, l_i = _banded_attention(slopes, no_sink, src, pos, dil=dil, radius=A_RADIUS, groups=_A_GROUPS,
                                         q_col=col0, q_w=GROUP_W, k_col=col0 + GROUP_W, k_w=GROUP_W,
                                         v_col=col0 + 2 * GROUP_W,
                                         with_lse=True, name="dilated_attention_%d" % dil)
            oa.append(o_i)
            la.append(l_i)
        (ob,) = _banded_attention(slopes, sink_logits[layer], z, pos, dil=1, radius=B_RADIUS, groups=_B_GROUPS,
                                  q_col=Z_QB, q_w=GROUP_W, k_col=Z_KB, k_w=LANE, v_col=Z_VB,
                                  with_lse=False, name="windowed_attention")
        oc = _diff_attention(slopes, stats, r3(lam_q1), r3(lam_k1), r3(lam_q2), r3(lam_k2), g_diff2, z,
                             pos_q_rows, pos_k_lanes, layer)
        od = _mla_attention(z)
        x = _mix_ffn(x, oa, la, ob, oc, od, mod, r3(g_post_mix), r3(g_pre_ffn), r3(g_post_ffn),
                     w_out_b, w_gu_b, w_down_b, layer)
    return x
```

```python
import functools
import math

import numpy as np
import jax
import jax.numpy as jnp
from jax import lax
from jax.experimental import pallas as pl
from jax.experimental.pallas import tpu as pltpu

F32 = jnp.float32
BF16 = jnp.bfloat16

D_MODEL = 1024
HEAD_DIM = 64
N_HEADS = 4
GROUP_W = N_HEADS * HEAD_DIM
DILATIONS = (1, 4, 16)
A_RADIUS = 64
B_RADIUS = 128
C_QK = 32
D_Q_RANK = 384
D_KV_RANK = 128
D_ROPE = 32
D_FF = 2816
ROPE_THETA = 10000.0
N_ALIBI = 12
EPS = 1e-6
NEG = -1e30
MASK_DIST = 1e30
LANE = 128

RAW_D = 768
RAW_PASS = 2176
RAW_W = RAW_D + RAW_PASS
Z_QD, Z_KD, Z_VD = 0, 512, 1024
Z_QA, Z_KA, Z_VA = 1280, 1536, 1792
Z_QB, Z_VB = 2048, 2304
Z_QC, Z_KC, Z_VC = 2560, 2816, 3072
Z_KB = 3328
Z_PAD = 3456
Z_W = 3584

TM_PROJ = 512
TQ_DIFF = 256
TQ_MLA = 512
TK_DENSE = 512
N_STREAMS = 4
V_ROWS = HEAD_DIM + 16
LOG2E = 1.4426950408889634
FOLD_LO = 64.0
FOLD_RANGE = 16384
AUG_ROWS = 16
BAND_BLK = 128
BAND_UNROLL = 8
BAND_MIN_BLOCKS = 8
VMEM_LIMIT = 56 * 1024 * 1024


def _cparams(sem):
    return pltpu.CompilerParams(dimension_semantics=sem, vmem_limit_bytes=VMEM_LIMIT)


def _rms(x, g):
    ms = jnp.mean(x * x, axis=-1, keepdims=True)
    return x * lax.rsqrt(ms + EPS) * g


def _dot(a, b):
    return jnp.dot(a, b, preferred_element_type=F32)


def _dot_nt(a, b):
    return lax.dot_general(a, b, (((1,), (1,)), ((), ())), preferred_element_type=F32)


def _lane_iota(w):
    return lax.broadcasted_iota(jnp.int32, (1, w), 1)


def _lane_mask(w, lo, hi):
    lane = _lane_iota(w)
    return (lane >= lo) & (lane < hi)


def _ada_kernel(c_ref, w_ref, b_ref, o_ref):
    c = c_ref[...]
    c_act = c / (1.0 + jnp.exp(-c))
    o_ref[0] = jnp.dot(c_act, w_ref[0], precision=lax.Precision.HIGHEST,
                       preferred_element_type=F32) + b_ref[0]


def _ada_modulation(c, w_ada, b_ada):
    n_layers, d, six_d = w_ada.shape
    bsz = c.shape[0]
    tn = 1024
    return pl.pallas_call(
        _ada_kernel,
        grid=(n_layers, six_d // tn),
        in_specs=[pl.BlockSpec((bsz, d), lambda l, j: (0, 0)),
                  pl.BlockSpec((1, d, tn), lambda l, j: (l, 0, j)),
                  pl.BlockSpec((1, 1, tn), lambda l, j: (l, 0, j))],
        out_specs=pl.BlockSpec((1, bsz, tn), lambda l, j: (l, 0, j)),
        out_shape=jax.ShapeDtypeStruct((n_layers, bsz, six_d), F32),
        compiler_params=_cparams(("arbitrary", "arbitrary")),
        name="ada_modulation",
    )(c, w_ada, b_ada.reshape(n_layers, 1, six_d))


def _rope_kernel(p_ref, inv_ref, cos_ref, sin_ref):
    ang = p_ref[0] * inv_ref[...]
    cos_ref[0] = jnp.cos(ang)
    sin_ref[0] = jnp.sin(ang)


def _rope_tables(pos_col, inv_full):
    bsz, t, _ = pos_col.shape
    tm = 1024
    spec = pl.BlockSpec((1, tm, LANE), lambda b, i: (b, i, 0))
    return pl.pallas_call(
        _rope_kernel,
        grid=(bsz, t // tm),
        in_specs=[pl.BlockSpec((1, tm, 1), lambda b, i: (b, i, 0)),
                  pl.BlockSpec((1, LANE), lambda b, i: (0, 0))],
        out_specs=[spec, spec],
        out_shape=[jax.ShapeDtypeStruct((bsz, t, LANE), F32)] * 2,
        compiler_params=_cparams(("arbitrary", "arbitrary")),
        name="rope_tables",
    )(pos_col, inv_full)


def _inproj_kernel(x_ref, mod_ref, g_ref, w1_ref, cs_ref, gq_ref, gkv_ref, wq2_ref, wkv2_ref,
                   cos_ref, sin_ref, z_ref, zf4_ref, zf16_ref, fold_ref, *, d_scale):
    x = x_ref[0]
    tm = x.shape[0]
    mod = mod_ref[0]
    shift, scale = mod[:, 0:D_MODEL], mod[:, D_MODEL:2 * D_MODEL]
    h = (_rms(x, g_ref[...]) * (1.0 + scale) + shift).astype(BF16)
    raw = _dot(h, w1_ref[...])
    cq = raw[:, 0:D_Q_RANK]
    ckv = raw[:, D_Q_RANK:D_Q_RANK + D_KV_RANK]
    kr_a = raw[:, 512:640]
    kr_b = raw[:, 640:768]
    cos = cos_ref[0]
    sin = sin_ref[0]
    cos4 = jnp.concatenate([cos] * N_HEADS, axis=1)
    sin4 = jnp.concatenate([sin] * N_HEADS, axis=1)
    q2 = _dot(_rms(cq, gq_ref[...]).astype(BF16), wq2_ref[...])
    qd = (q2[:, 0:512] * cos4 + q2[:, 512:1024] * sin4) * d_scale
    kv2 = _dot(_rms(ckv, gkv_ref[...]).astype(BF16), wkv2_ref[...])
    kr = kr_a * cos + kr_b * sin
    kd = kv2[:, 0:512] + jnp.concatenate([kr] * N_HEADS, axis=1)
    z_ref[0, :, Z_QD:Z_QD + 512] = qd.astype(BF16)
    z_ref[0, :, Z_KD:Z_KD + 512] = kd.astype(BF16)
    z_ref[0, :, Z_VD:Z_VD + 256] = kv2[:, 512:768].astype(BF16)
    passed = raw[:, RAW_D:RAW_W] * cs_ref[...]
    z_ref[0, :, Z_QA:Z_PAD] = passed.astype(BF16)
    z_ref[0, :, Z_PAD:Z_W] = jnp.zeros((tm, Z_W - Z_PAD), BF16)
    n_slab = 3 * GROUP_W // LANE
    for s in range(n_slab):
        fold_ref[s] = passed[:, s * LANE:(s + 1) * LANE]
    for dil, out_ref in ((DILATIONS[1], zf4_ref), (DILATIONS[2], zf16_ref)):
        for r in range(dil):
            for s in range(n_slab):
                out_ref[0, r, :, s * LANE:(s + 1) * LANE] = (
                    fold_ref[s, pl.ds(r, tm // dil, stride=dil), :].astype(BF16))


def _input_projection(x, mod, g_pre, w1, col_scale, g_q, g_kv, wq2, wkv2, cos_t, sin_t, layer):
    bsz, t, d = x.shape
    tm = TM_PROJ
    const = lambda b, i: (0, 0)
    lsel = lambda b, i: (layer, 0, 0)
    kern = functools.partial(_inproj_kernel, d_scale=(HEAD_DIM + D_ROPE) ** -0.5 * LOG2E)
    return pl.pallas_call(
        kern,
        grid=(bsz, t // tm),
        in_specs=[pl.BlockSpec((1, tm, d), lambda b, i: (b, i, 0)),
                  pl.BlockSpec((None, 1, 1, 6 * d), lambda b, i: (layer, b, 0, 0)),
                  pl.BlockSpec((None, 1, d), lsel),
                  pl.BlockSpec((None, d, RAW_W), lsel),
                  pl.BlockSpec((1, RAW_PASS), const),
                  pl.BlockSpec((None, 1, D_Q_RANK), lsel),
                  pl.BlockSpec((None, 1, D_KV_RANK), lsel),
                  pl.BlockSpec((None, D_Q_RANK, 1024), lsel),
                  pl.BlockSpec((None, D_KV_RANK, 768), lsel),
                  pl.BlockSpec((1, tm, LANE), lambda b, i: (b, i, 0)),
                  pl.BlockSpec((1, tm, LANE), lambda b, i: (b, i, 0))],
        out_specs=[pl.BlockSpec((1, tm, Z_W), lambda b, i: (b, i, 0))] +
                  [pl.BlockSpec((1, dil, tm // dil, 3 * GROUP_W), lambda b, i: (b, 0, i, 0))
                   for dil in DILATIONS[1:]],
        out_shape=[jax.ShapeDtypeStruct((bsz, t, Z_W), BF16)] +
                  [jax.ShapeDtypeStruct((bsz, dil, t // dil, 3 * GROUP_W), BF16) for dil in DILATIONS[1:]],
        scratch_shapes=[pltpu.VMEM((3 * GROUP_W // LANE, tm, LANE), F32)],
        compiler_params=_cparams(("arbitrary", "arbitrary")),
        name="input_projection",
    )(x, mod, g_pre, w1, col_scale, g_q, g_kv, wq2, wkv2, cos_t, sin_t)


def _banded_kernel(sl_ref, sink_ref, q_ref, k_ref, v_ref, pk_ref, pq_ref, *rest,
                   n, dil, radius, groups, with_lse):
    if with_lse:
        o_ref, lse_ref, vm_ref = rest
    else:
        o_ref, vm_ref = rest
        lse_ref = None
    blk = BAND_BLK
    win = min(blk + 2 * radius, n)
    n_g = len(groups)
    n_res = q_ref.shape[0]
    lane_o = _lane_iota(GROUP_W)
    for rl in range(n_res):
        v_all = v_ref[rl]
        for g, grp in enumerate(groups):
            vmask = (lane_o >= grp[4]) & (lane_o < grp[5])
            vm_ref[rl, g] = jnp.where(vmask, v_all, jnp.zeros_like(v_all))
    stat_row = lax.broadcasted_iota(jnp.int32, (LANE, 1), 0)

    def one_block(rl, i):
        r0 = pl.multiple_of(i * blk, blk)
        start = pl.multiple_of(jnp.clip(r0 - radius, 0, n - win), radius)
        q = q_ref[rl, pl.ds(r0, blk), :].astype(F32)
        q_t = {}
        parts = []
        for (q_lo, q_w, qm_lo, qm_hi, _, _, _, _) in groups:
            if q_lo not in q_t:
                q_t[q_lo] = q[:, q_lo:q_lo + q_w].T
            dim = lax.broadcasted_iota(jnp.int32, (q_w, 1), 0)
            parts.append(jnp.where((dim >= qm_lo) & (dim < qm_hi), q_t[q_lo], 0.0))
        qs_t = jnp.concatenate(parts, axis=1).astype(BF16)
        s_t = _dot(k_ref[rl, pl.ds(start, win), :], qs_t)
        dist = jnp.abs(pk_ref[rl, pl.ds(start, win), :] - pq_ref[rl, i])
        ki = start + lax.broadcasted_iota(jnp.int32, (win, 1), 0)
        qi = r0 + lax.broadcasted_iota(jnp.int32, (1, blk), 1)
        dist = jnp.where(jnp.abs(qi - ki) <= radius, dist, MASK_DIST)
        p_parts = []
        inv_rows = jnp.zeros((LANE, blk), F32)
        lse_rows = jnp.zeros((LANE, blk), F32)
        for g, grp in enumerate(groups):
            sg = s_t[:, g * blk:(g + 1) * blk] - (sl_ref[grp[6]] * LOG2E) * dist
            m = jnp.max(sg, axis=0, keepdims=True)
            if grp[7] is not None:
                sk = sink_ref[grp[7]] * LOG2E
                m = jnp.maximum(m, sk)
            p = jnp.exp2(sg - m)
            den = jnp.sum(p, axis=0, keepdims=True)
            if grp[7] is not None:
                den = den + jnp.exp2(sk - m)
            p_parts.append(p.T.astype(BF16))
            inv_rows = jnp.where(stat_row == g, 1.0 / den, inv_rows)
            lse_rows = jnp.where(stat_row == g, m + jnp.log2(den), lse_rows)
        pcat = jnp.concatenate(p_parts, axis=1)
        vcat = jnp.concatenate([vm_ref[rl, g, pl.ds(start, win), :] for g in range(n_g)], axis=0)
        o = _dot(pcat, vcat)
        inv_cols = inv_rows.T
        lse_cols = lse_rows.T
        inv_full = jnp.zeros((blk, GROUP_W), F32)
        lse_full = jnp.zeros((blk, GROUP_W), F32)
        for g, grp in enumerate(groups):
            vmask = (lane_o >= grp[4]) & (lane_o < grp[5])
            inv_full = jnp.where(vmask, inv_cols[:, g:g + 1], inv_full)
            lse_full = jnp.where(vmask, lse_cols[:, g:g + 1], lse_full)
        o = o * inv_full
        if not with_lse:
            o_ref[pl.ds(r0, blk), :] = o.astype(o_ref.dtype)
            return
        if dil == 1:
            rows = pl.ds(r0, blk)
        else:
            rows = pl.ds(r0 * dil + pl.program_id(1) * n_res + rl, blk, stride=dil)
        for s in range(GROUP_W // LANE):
            o_ref[s, rows, :] = o[:, s * LANE:(s + 1) * LANE]
            lse_ref[s, rows, :] = lse_full[:, s * LANE:(s + 1) * LANE]

    unroll = min(BAND_UNROLL, n // blk)
    trips = n // (blk * unroll)
    for rl in range(n_res):
        def body(j, carry, rl=rl):
            for u in range(unroll):
                one_block(rl, j * unroll + u)
            return carry

        if trips == 1:
            body(0, 0)
        else:
            lax.fori_loop(0, trips, body, 0)


def _banded_attention(slopes, sink, src, pos, *, dil, radius, groups, q_col, q_w, k_col, k_w, v_col,
                      with_lse, name):
    bsz, t = pos.shape
    n = t // dil
    blk = BAND_BLK
    n_res = max(1, min(dil, BAND_MIN_BLOCKS // (n // blk)))
    if dil == 1:
        src = src.reshape(bsz, 1, t, src.shape[-1])
    pos_f = pos.reshape(bsz, n, dil).transpose(0, 2, 1)
    pos_k = jnp.broadcast_to(pos_f[..., None], (bsz, dil, n, LANE))
    pos_q = pos_f.reshape(bsz, dil, n // blk, 1, blk)
    kern = functools.partial(_banded_kernel, n=n, dil=dil, radius=radius, groups=groups, with_lse=with_lse)
    if with_lse:
        o_spec = pl.BlockSpec((None, GROUP_W // LANE, t, LANE), lambda b, r: (b, 0, 0, 0))
        out_specs = [o_spec, o_spec]
        out_shape = [jax.ShapeDtypeStruct((bsz, GROUP_W // LANE, t, LANE), F32)] * 2
    else:
        out_specs = [pl.BlockSpec((None, t, GROUP_W), lambda b, r: (b, 0, 0))]
        out_shape = [jax.ShapeDtypeStruct((bsz, t, GROUP_W), BF16)]
    smem = pl.BlockSpec(memory_space=pltpu.SMEM)

    def zspec(col, w):
        return pl.BlockSpec((None, n_res, n, w), lambda b, r: (b, r, 0, col // w))

    return pl.pallas_call(
        kern,
        grid=(bsz, dil // n_res),
        in_specs=[smem, smem, zspec(q_col, q_w), zspec(k_col, k_w), zspec(v_col, GROUP_W),
                  pl.BlockSpec((None, n_res, n, LANE), lambda b, r: (b, r, 0, 0)),
                  pl.BlockSpec((None, n_res, n // blk, 1, blk), lambda b, r: (b, r, 0, 0, 0))],
        out_specs=out_specs,
        out_shape=out_shape,
        scratch_shapes=[pltpu.VMEM((n_res, len(groups), n, GROUP_W), BF16)],
        compiler_params=_cparams(("arbitrary", "arbitrary")),
        name=name,
    )(slopes, sink, src, src, src, pos_k, pos_q)


def _cat_lanes(xs):
    return xs[0] if len(xs) == 1 else jnp.concatenate(xs, axis=1)


def _flash_loop(n_kb, streams, vs_fn, n_maps, tq, tk):
    n = n_maps * tq
    n_g = 2 * n_maps
    hd = HEAD_DIM
    pv_w = 2 * LANE
    n_pv = 2 * (n // pv_w)

    def scores_part(tile_fn, s_ref, g):
        s = tile_fn()
        s_ref[:, g * tq:(g + 1) * tq] = s
        return jnp.max(s, axis=0, keepdims=True)

    def softmax_part(s_ref, p_ref, m, col_max, g):
        mn = jnp.maximum(m[g], col_max[g])
        h, mp = divmod(g, n_maps)
        p_ref[h * tk:(h + 1) * tk, mp * tq:(mp + 1) * tq] = (
            jnp.exp2(s_ref[:, g * tq:(g + 1) * tq] - mn).astype(BF16))
        return mn, jnp.exp2(m[g] - mn)

    def pv_part(kb, p_ref, acc_ref, alphas, piece):
        h, j = divmod(piece, n // pv_w)
        c = slice(j * pv_w, (j + 1) * pv_w)
        pv = _dot(vs_fn(kb, h), p_ref[h * tk:(h + 1) * tk, c])
        acc_ref[h, :, c] = alphas[h][:, c] * acc_ref[h, :, c] + pv

    def pack_alphas(alphas):
        return _cat_lanes(alphas[0:n_maps]), _cat_lanes(alphas[n_maps:n_g])

    def scores_stage(kb, stream):
        scores_fn, fix_fn, s_refs, _, _ = stream
        tiles = scores_fn(kb)
        col_max = [scores_part(tiles[g], s_refs[0], g) for g in range(n_g)]
        return col_max if fix_fn is None else fix_fn(kb, s_refs[0], col_max)

    def step(kb, cur, state):
        nxt = 1 - cur
        kb_next = jnp.minimum(kb + 1, n_kb - 1)
        kb_prev = jnp.maximum(kb - 1, 0)
        tiles = [stream[0](kb_next) for stream in streams]
        new = [([], [], []) for _ in streams]
        for g in range(n_g):
            for (_, _, s_refs, p_refs, acc_ref), (m, col_max, a_prev), (m_new, alphas, col_max_next), tl in zip(
                    streams, state, new, tiles):
                mn, alpha = softmax_part(s_refs[cur], p_refs[cur], m, col_max, g)
                m_new.append(mn)
                alphas.append(alpha)
                col_max_next.append(scores_part(tl[g], s_refs[nxt], g))
                for piece in range(g * n_pv // n_g, (g + 1) * n_pv // n_g):
                    pv_part(kb_prev, p_refs[nxt], acc_ref, a_prev, piece)
        out = []
        for (_, fix_fn, s_refs, _, _), (m_new, alphas, col_max_next) in zip(streams, new):
            if fix_fn is not None:
                col_max_next = fix_fn(kb_next, s_refs[nxt], col_max_next)
            out.append((m_new, col_max_next, pack_alphas(alphas)))
        return out

    def body(i, state):
        return step(2 * i + 1, 1, step(2 * i, 0, state))

    init = []
    for stream in streams:
        stream[4][...] = jnp.zeros((2, V_ROWS, n), F32)
        stream[3][1][...] = jnp.zeros((2 * tk, n), BF16)
        init.append(([jnp.full((1, tq), NEG, F32)] * n_g, scores_stage(0, stream),
                     (jnp.ones((1, n), F32), jnp.ones((1, n), F32))))
    state = lax.fori_loop(0, n_kb // 2, body, init)
    results = []
    for (_, _, _, p_refs, acc_ref), (_, _, a_last) in zip(streams, state):
        for piece in range(n_pv):
            pv_part(n_kb - 1, p_refs[1], acc_ref, a_last, piece)
        results.append((acc_ref[0, 0:hd], acc_ref[1, 0:hd], acc_ref[0, hd:hd + 1], acc_ref[1, hd:hd + 1]))
    return results


def _flash_scratch(n_maps, tq, tk):
    n = n_maps * tq
    return ([pltpu.VMEM((tk, 2 * n), F32)] * 2 + [pltpu.VMEM((2 * tk, n), BF16)] * 2
            + [pltpu.VMEM((2, V_ROWS, n), F32)])


def _fill_vt(v_ref, vt_ref, t, tk):
    v_t = v_ref[0].astype(F32).T
    row = lax.broadcasted_iota(jnp.int32, (V_ROWS - HEAD_DIM, tk), 0)
    ones = jnp.where(row == 0, 1.0, 0.0)
    for kb in range(t // tk):
        for h in range(2):
            vals = v_t[h * HEAD_DIM:(h + 1) * HEAD_DIM, kb * tk:(kb + 1) * tk]
            vt_ref[h, kb] = jnp.concatenate([vals, ones], axis=0).astype(BF16)


def _split3(x):
    x1 = x.astype(BF16).astype(F32)
    x2 = (x - x1).astype(BF16).astype(F32)
    x3 = (x - x1 - x2).astype(BF16).astype(F32)
    return x1, x2, x3


def _diff_kernel(sl_ref, base_ref, qlo_ref, qhi_ref, klo_ref, khi_ref, lq1_ref, lk1_ref, lq2_ref, lk2_ref,
                 gd_ref, q_ref, k_ref, v_ref, pq_ref, pk_ref, o_ref, vt_ref, kx_ref, *flash_scratch,
                 t, lam_init):
    tq, tk = TQ_DIFF, TK_DENSE
    n_q, n_k = t // tq, t // tk
    b = pl.program_id(0)
    hp = pl.program_id(1)
    qi = pl.program_id(2)
    base = base_ref[b]
    base_f = base.astype(F32)
    lane = _lane_iota(LANE)

    @pl.when(qi == 0)
    def _():
        _fill_vt(v_ref, vt_ref, t, tk)
        p_rel = pk_ref[0] - base_f
        p_hi = jnp.floor(p_rel * (1.0 / FOLD_LO)) * FOLD_LO
        p_lo = p_rel - p_hi
        aug = jnp.where(lane < 3, p_hi, jnp.where(lane < 6, p_lo, jnp.where(lane < 9, 1.0, 0.0)))
        kx_ref[:, 0:LANE] = k_ref[0]
        kx_ref[:, LANE:2 * LANE] = aug.astype(BF16)

    slopes2 = [sl_ref[N_HEADS + 2 * hp] * LOG2E, sl_ref[N_HEADS + 2 * hp + 1] * LOG2E]
    row = lax.broadcasted_iota(jnp.int32, (LANE, 1), 0)
    arow = lax.broadcasted_iota(jnp.int32, (AUG_ROWS, 1), 0)
    pad_rows = jnp.zeros((LANE - AUG_ROWS, 4 * tq), BF16)

    def make_stream(st, scratch):
        q_t = q_ref[0, st * tq:(st + 1) * tq, :].astype(F32).T
        qs_t = jnp.concatenate(
            [jnp.where((row >= C_QK * g) & (row < C_QK * (g + 1)), q_t, 0.0) for g in range(4)],
            axis=1).astype(BF16)
        pq = pq_ref[0, st]
        aug_parts = []
        for h in range(2):
            m1, m2, m3 = _split3(jnp.full((1, tq), slopes2[h], F32))
            w1, w2, w3 = _split3(-slopes2[h] * (pq - base_f))
            blk = jnp.where((arow == 0) | (arow == 3), m1,
                  jnp.where((arow == 1) | (arow == 4), m2,
                  jnp.where((arow == 2) | (arow == 5), m3,
                  jnp.where(arow == 6, w1, jnp.where(arow == 7, w2, jnp.where(arow == 8, w3, 0.0))))))
            aug_parts += [blk, blk]
        aug_plus = jnp.concatenate(aug_parts, axis=1)
        q_idx = b * n_q + N_STREAMS * qi + st
        q_hi = qhi_ref[q_idx]
        q_lo = qlo_ref[q_idx]
        q_ok = q_hi - base < FOLD_RANGE

        def sign_of(kb):
            k_lo = klo_ref[b * n_k + kb]
            k_hi = khi_ref[b * n_k + kb]
            ok = q_ok & (k_hi - base < FOLD_RANGE)
            return jnp.where(ok & (k_hi <= q_lo), 1, jnp.where(ok & (k_lo >= q_hi), -1, 0))

        def scores(kb):
            off = pl.multiple_of(kb * tk, tk)
            kk = kx_ref[pl.ds(off, tk), :]
            sg = sign_of(kb).astype(F32)
            rhs = jnp.concatenate([qs_t, (sg * aug_plus).astype(BF16), pad_rows], axis=0)
            return [functools.partial(lambda g: _dot(kk, rhs[:, g * tq:(g + 1) * tq]), g)
                    for g in range(4)]

        def fix(kb, s_ref, col_max):
            def explicit():
                off = pl.multiple_of(kb * tk, tk)
                pk = pk_ref[0, pl.ds(off, tk), :]
                dist = jnp.abs(jnp.concatenate([pk] * (tq // LANE), axis=1) - pq)
                out = []
                for g in range(4):
                    s = s_ref[:, g * tq:(g + 1) * tq] - slopes2[g // 2] * dist
                    s_ref[:, g * tq:(g + 1) * tq] = s
                    out.append(jnp.max(s, axis=0, keepdims=True))
                return out
            return lax.cond(sign_of(kb) == 0, explicit, lambda: col_max)

        return (scores, fix, scratch[0:2], scratch[2:4], scratch[4])

    streams = [make_stream(st, flash_scratch[5 * st:5 * st + 5]) for st in range(N_STREAMS)]
    results = _flash_loop(n_k, streams, lambda kb, h: vt_ref[h, kb], 2, tq, tk)
    lam = (jnp.exp(jnp.sum(lq1_ref[...] * lk1_ref[...], axis=1, keepdims=True))
           - jnp.exp(jnp.sum(lq2_ref[...] * lk2_ref[...], axis=1, keepdims=True)) + lam_init)
    for st, (acc0, acc1, l0, l1) in enumerate(results):
        o0 = acc0 / l0
        o1 = acc1 / l1
        o_t = jnp.concatenate([o0[:, 0:tq] - lam * o0[:, tq:2 * tq],
                               o1[:, 0:tq] - lam * o1[:, tq:2 * tq]], axis=0)
        o = o_t.T
        o_sq = o * o
        ms0 = jnp.sum(jnp.where(lane < HEAD_DIM, o_sq, 0.0), axis=1, keepdims=True) * (1.0 / HEAD_DIM)
        ms1 = jnp.sum(jnp.where(lane >= HEAD_DIM, o_sq, 0.0), axis=1, keepdims=True) * (1.0 / HEAD_DIM)
        ms = jnp.where(lane < HEAD_DIM, ms0, ms1)
        y = o * lax.rsqrt(ms + EPS) * gd_ref[...]
        o_ref[0, st * tq:(st + 1) * tq, :] = (y * (1.0 - lam_init)).astype(o_ref.dtype)


def _block_stats(positions):
    bsz, t = positions.shape
    pq = positions.reshape(bsz, t // TQ_DIFF, TQ_DIFF)
    pk = positions.reshape(bsz, t // TK_DENSE, TK_DENSE)
    return (jnp.min(positions, axis=1), jnp.min(pq, axis=2).reshape(-1), jnp.max(pq, axis=2).reshape(-1),
            jnp.min(pk, axis=2).reshape(-1), jnp.max(pk, axis=2).reshape(-1))


def _diff_attention(slopes, stats, lq1, lk1, lq2, lk2, g_diff2, z, pos_q_rows, pos_k_lanes, layer):
    bsz, t, _ = z.shape
    tq, tk = TQ_DIFF, TK_DENSE
    lam_init = 0.8 - 0.6 * math.exp(-0.3 * layer)
    kern = functools.partial(_diff_kernel, t=t, lam_init=lam_init)
    lsel = lambda b, h, i: (layer, 0, 0)
    smem = pl.BlockSpec(memory_space=pltpu.SMEM)
    return pl.pallas_call(
        kern,
        grid=(bsz, 2, t // (N_STREAMS * tq)),
        in_specs=[smem] * 6 +
                 [pl.BlockSpec((None, 1, C_QK), lsel), pl.BlockSpec((None, 1, C_QK), lsel),
                  pl.BlockSpec((None, 1, C_QK), lsel), pl.BlockSpec((None, 1, C_QK), lsel),
                  pl.BlockSpec((None, 1, LANE), lsel),
                  pl.BlockSpec((1, N_STREAMS * tq, LANE), lambda b, h, i: (b, i, Z_QC // LANE + h)),
                  pl.BlockSpec((1, t, LANE), lambda b, h, i: (b, 0, Z_KC // LANE + h)),
                  pl.BlockSpec((1, t, LANE), lambda b, h, i: (b, 0, Z_VC // LANE + h)),
                  pl.BlockSpec((1, N_STREAMS, 1, tq), lambda b, h, i: (b, i, 0, 0)),
                  pl.BlockSpec((1, t, LANE), lambda b, h, i: (b, 0, 0))],
        out_specs=pl.BlockSpec((1, N_STREAMS * tq, LANE), lambda b, h, i: (b, i, h)),
        out_shape=jax.ShapeDtypeStruct((bsz, t, GROUP_W), BF16),
        scratch_shapes=[pltpu.VMEM((2, t // tk, V_ROWS, tk), BF16), pltpu.VMEM((t, 2 * LANE), BF16)]
                       + _flash_scratch(2, tq, tk) * N_STREAMS,
        compiler_params=_cparams(("arbitrary", "arbitrary", "arbitrary")),
        name="diff_attention",
    )(slopes, *stats, lq1, lk1, lq2, lk2, g_diff2, z, z, z, pos_q_rows, pos_k_lanes)


def _mla_kernel(q_ref, k_ref, v_ref, o_ref, vt_ref, *flash_scratch, t):
    tq, tk = TQ_MLA, TK_DENSE

    @pl.when(pl.program_id(2) == 0)
    def _():
        _fill_vt(v_ref, vt_ref, t, tk)

    def make_stream(st, scratch):
        q = q_ref[0, st * tq:(st + 1) * tq, :].astype(F32)
        q_t = [q[:, h * LANE:(h + 1) * LANE].T.astype(BF16) for h in range(2)]

        def scores(kb):
            off = pl.multiple_of(kb * tk, tk)
            kk = k_ref[0, pl.ds(off, tk), :]
            return [functools.partial(lambda h: _dot(kk[:, h * LANE:(h + 1) * LANE], q_t[h]), h)
                    for h in range(2)]

        return (scores, None, scratch[0:2], scratch[2:4], scratch[4])

    streams = [make_stream(st, flash_scratch[5 * st:5 * st + 5]) for st in range(N_STREAMS)]
    results = _flash_loop(t // tk, streams, lambda kb, h: vt_ref[h, kb], 1, tq, tk)
    for st, (acc0, acc1, l0, l1) in enumerate(results):
        o_t = jnp.concatenate([acc0 / l0, acc1 / l1], axis=0)
        o_ref[0, st * tq:(st + 1) * tq, :] = o_t.T.astype(o_ref.dtype)


def _mla_attention(z):
    bsz, t, _ = z.shape
    tq, tk = TQ_MLA, TK_DENSE
    kern = functools.partial(_mla_kernel, t=t)
    return pl.pallas_call(
        kern,
        grid=(bsz, 2, t // (N_STREAMS * tq)),
        in_specs=[pl.BlockSpec((1, N_STREAMS * tq, 2 * LANE), lambda b, h, i: (b, i, Z_QD // 256 + h)),
                  pl.BlockSpec((1, t, 2 * LANE), lambda b, h, i: (b, 0, Z_KD // 256 + h)),
                  pl.BlockSpec((1, t, LANE), lambda b, h, i: (b, 0, Z_VD // LANE + h))],
        out_specs=pl.BlockSpec((1, N_STREAMS * tq, LANE), lambda b, h, i: (b, i, h)),
        out_shape=jax.ShapeDtypeStruct((bsz, t, GROUP_W), BF16),
        scratch_shapes=[pltpu.VMEM((2, t // tk, V_ROWS, tk), BF16)] + _flash_scratch(1, tq, tk) * N_STREAMS,
        compiler_params=_cparams(("arbitrary", "arbitrary", "arbitrary")),
        name="mla_attention",
    )(z, z, z)


def _mix_ffn_kernel(x_ref, oa1_ref, oa2_ref, oa3_ref, la1_ref, la2_ref, la3_ref, ob_ref, oc_ref, od_ref,
                    mod_ref, gpm_ref, gpf_ref, gqf_ref, wout_ref, wgu_ref, wd_ref, out_ref, a_ref):
    d = D_MODEL
    x = x_ref[0]
    mod = mod_ref[0]
    gate_m = mod[:, 2 * d:3 * d]
    shift_f, scale_f, gate_f = mod[:, 3 * d:4 * d], mod[:, 4 * d:5 * d], mod[:, 5 * d:6 * d]
    slabs = lambda ref: jnp.concatenate([ref[0, s] for s in range(GROUP_W // LANE)], axis=1)
    l1, l2, l3 = slabs(la1_ref), slabs(la2_ref), slabs(la3_ref)
    mx = jnp.maximum(jnp.maximum(l1, l2), l3)
    e1, e2, e3 = jnp.exp2(l1 - mx), jnp.exp2(l2 - mx), jnp.exp2(l3 - mx)
    mix_a = (e1 * slabs(oa1_ref) + e2 * slabs(oa2_ref) + e3 * slabs(oa3_ref)) / (e1 + e2 + e3)
    mix = jnp.concatenate([mix_a.astype(BF16), ob_ref[0], oc_ref[0], od_ref[0]], axis=1)
    y = _dot(mix, wout_ref[...])
    x1 = x + gate_m * _rms(y, gpm_ref[...])
    h = (_rms(x1, gpf_ref[...]) * (1.0 + scale_f) + shift_f).astype(BF16)
    ch = 256
    for c0 in range(0, D_FF, ch):
        g = _dot(h, wgu_ref[:, c0:c0 + ch])
        u = _dot(h, wgu_ref[:, D_FF + c0:D_FF + c0 + ch])
        a_ref[:, c0:c0 + ch] = (g / (1.0 + jnp.exp(-g)) * u).astype(BF16)
    y2 = _dot(a_ref[...], wd_ref[...])
    out_ref[0] = x1 + gate_f * _rms(y2, gqf_ref[...])


def _mix_ffn(x, oa, la, ob, oc, od, mod, g_post_mix, g_pre_ffn, g_post_ffn, w_out, w_gu, w_down, layer):
    bsz, t, d = x.shape
    tm = TM_PROJ
    tok = lambda w: pl.BlockSpec((1, tm, w), lambda b, i: (b, i, 0))
    lsel = lambda b, i: (layer, 0, 0)
    once = dict(pipeline_mode=pl.Buffered(1))
    return pl.pallas_call(
        _mix_ffn_kernel,
        grid=(bsz, t // tm),
        in_specs=[tok(d)] +
                 [pl.BlockSpec((1, GROUP_W // LANE, tm, LANE), lambda b, i: (b, 0, i, 0))] * 6 +
                 [tok(GROUP_W)] * 3 +
                 [pl.BlockSpec((None, 1, 1, 6 * d), lambda b, i: (layer, b, 0, 0)),
                  pl.BlockSpec((None, 1, d), lsel), pl.BlockSpec((None, 1, d), lsel),
                  pl.BlockSpec((None, 1, d), lsel),
                  pl.BlockSpec((None, d, d), lsel, **once),
                  pl.BlockSpec((None, d, 2 * D_FF), lsel, **once),
                  pl.BlockSpec((None, D_FF, d), lsel, **once)],
        out_specs=tok(d),
        out_shape=jax.ShapeDtypeStruct((bsz, t, d), F32),
        scratch_shapes=[pltpu.VMEM((tm, D_FF), BF16)],
        compiler_params=_cparams(("arbitrary", "arbitrary")),
        name="mix_ffn",
    )(x, *oa, *la, ob, oc, od, mod, g_post_mix, g_pre_ffn, g_post_ffn, w_out, w_gu, w_down)


def _rot_cols(w):
    half = w.shape[-1] // 2
    return jnp.concatenate([-w[..., half:], w[..., :half]], axis=-1)


def _prep_w_in(w_in):
    n_layers, d, _ = w_in.shape
    z64 = jnp.zeros((n_layers, d, 64), w_in.dtype)
    z32 = jnp.zeros((n_layers, d, 32), w_in.dtype)
    col = lambda a, b: w_in[:, :, a:b]
    kr = col(2560, 2592)
    qb = 768
    vb = 1152
    parts = [col(2048, 2432), col(2432, 2560),
             z64, kr, z32,
             z64, _rot_cols(kr), z32,
             col(0, 768),
             col(qb, qb + 64), col(qb + 128, qb + 192), col(qb + 64, qb + 128), col(qb + 192, qb + 256),
             col(vb, vb + 64), col(vb, vb + 64), col(vb + 64, vb + 128), col(vb + 64, vb + 128),
             col(1280, 2048),
             col(1024, 1152)]
    return jnp.concatenate(parts, axis=-1).astype(BF16)


def _prep_w_uq(w_uq):
    n_layers, r, _ = w_uq.shape
    z64 = jnp.zeros((n_layers, r, 64), w_uq.dtype)
    z32 = jnp.zeros((n_layers, r, 32), w_uq.dtype)
    main, rot = [], []
    for h in range(N_HEADS):
        base = h * (HEAD_DIM + D_ROPE)
        nope = w_uq[:, :, base:base + HEAD_DIM]
        rope = w_uq[:, :, base + HEAD_DIM:base + HEAD_DIM + D_ROPE]
        main += [nope, rope, z32]
        rot += [z64, _rot_cols(rope), z32]
    return jnp.concatenate(main + rot, axis=-1).astype(BF16)


def _prep_w_ukv(w_ukv):
    n_layers, r, _ = w_ukv.shape
    z64 = jnp.zeros((n_layers, r, 64), w_ukv.dtype)
    ks, vs = [], []
    for h in range(N_HEADS):
        base = h * 2 * HEAD_DIM
        ks += [w_ukv[:, :, base:base + HEAD_DIM], z64]
        vs.append(w_ukv[:, :, base + HEAD_DIM:base + 2 * HEAD_DIM])
    return jnp.concatenate(ks + vs, axis=-1).astype(BF16)


def _col_scale():
    cs = np.ones((1, RAW_PASS), np.float32)
    cs[0, Z_QA - Z_QA:Z_QA - Z_QA + 256] = HEAD_DIM ** -0.5 * LOG2E
    cs[0, Z_QB - Z_QA:Z_QB - Z_QA + 256] = HEAD_DIM ** -0.5 * LOG2E
    cs[0, Z_QC - Z_QA:Z_QC - Z_QA + 256] = C_QK ** -0.5 * LOG2E
    return jnp.asarray(cs)


_A_GROUPS = tuple((0, GROUP_W, 64 * h, 64 * h + 64, 64 * h, 64 * h + 64, 8 + h, None) for h in range(4))
_B_GROUPS = tuple((128 * r, 128, 64 * g, 64 * g + 64, 128 * g + 64 * r, 128 * g + 64 * r + 64, 2 * g + r, 2 * g + r)
                  for g in range(2) for r in range(2))


def kernel(x, c, positions, w_ada, b_ada, g_pre_mix, g_post_mix, w_in, sink_logits, lam_q1, lam_k1, lam_q2, lam_k2, g_diff, g_mla_q, g_mla_kv, w_uq, w_ukv, w_out, g_pre_ffn, g_post_ffn, w_gate_up, w_down):
    bsz, t, d = x.shape
    n_layers = w_in.shape[0]
    pos = positions.astype(F32)
    pos_col = pos.reshape(bsz, t, 1)
    pos_q_rows = pos.reshape(bsz, t // TQ_DIFF, 1, TQ_DIFF)
    pos_k_lanes = jnp.broadcast_to(pos[:, :, None], (bsz, t, LANE))
    stats = _block_stats(positions)
    j = jnp.arange(1, N_ALIBI + 1, dtype=F32)
    slopes = jnp.exp2(-8.0 * j / N_ALIBI)
    half = D_ROPE // 2
    inv = jnp.power(ROPE_THETA, -jnp.arange(half, dtype=F32) / half)
    inv_full = jnp.concatenate([jnp.zeros((64,), F32), inv, inv, jnp.zeros((32,), F32)]).reshape(1, LANE)

    w1 = _prep_w_in(w_in)
    wq2 = _prep_w_uq(w_uq)
    wkv2 = _prep_w_ukv(w_ukv)
    w_out_b = w_out.astype(BF16)
    w_gu_b = w_gate_up.astype(BF16)
    w_down_b = w_down.astype(BF16)
    col_scale = _col_scale()
    r3 = lambda a: a.reshape(n_layers, 1, a.shape[-1])
    g_diff2 = r3(jnp.concatenate([g_diff, g_diff], axis=-1))
    no_sink = jnp.zeros((1,), F32)

    mod = _ada_modulation(c, w_ada, b_ada).reshape(n_layers, bsz, 1, 6 * d)
    cos_t, sin_t = _rope_tables(pos_col, inv_full)

    for layer in range(n_layers):
        z, zf4, zf16 = _input_projection(x, mod, r3(g_pre_mix), w1, col_scale, r3(g_mla_q), r3(g_mla_kv),
                                         wq2, wkv2, cos_t, sin_t, layer)
        oa, la = [], []
        for dil, src, col0 in zip(DILATIONS, (z, zf4, zf16), (Z_QA, 0, 0)):
            o_i, l_i = _banded_attention(slopes, no_sink, src, pos, dil=dil, radius=A_RADIUS, groups=_A_GROUPS,
                                         q_col=col0, q_w=GROUP_W, k_col=col0 + GROUP_W, k_w=GROUP_W,
                                         v_col=col0 + 2 * GROUP_W,
                                         with_lse=True, name="dilated_attention_%d" % dil)
            oa.append(o_i)
            la.append(l_i)
        (ob,) = _banded_attention(slopes, sink_logits[layer], z, pos, dil=1, radius=B_RADIUS, groups=_B_GROUPS,
                                  q_col=Z_QB, q_w=GROUP_W, k_col=Z_KB, k_w=LANE, v_col=Z_VB,
                                  with_lse=False, name="windowed_attention")
        oc = _diff_attention(slopes, stats, r3(lam_q1), r3(lam_k1), r3(lam_q2), r3(lam_k2), g_diff2, z,
                             pos_q_rows, pos_k_lanes, layer)
        od = _mla_attention(z)
        x = _mix_ffn(x, oa, la, ob, oc, od, mod, r3(g_post_mix), r3(g_pre_ffn), r3(g_post_ffn),
                     w_out_b, w_gu_b, w_down_b, layer)
    return x
```

```python
import functools
import math

import numpy as np
import jax
import jax.numpy as jnp
from jax import lax
from jax.experimental import pallas as pl
from jax.experimental.pallas import tpu as pltpu

F32 = jnp.float32
BF16 = jnp.bfloat16

D_MODEL = 1024
HEAD_DIM = 64
N_HEADS = 4
GROUP_W = N_HEADS * HEAD_DIM
DILATIONS = (1, 4, 16)
A_RADIUS = 64
B_RADIUS = 128
C_QK = 32
D_Q_RANK = 384
D_KV_RANK = 128
D_ROPE = 32
D_FF = 2816
ROPE_THETA = 10000.0
N_ALIBI = 12
EPS = 1e-6
NEG = -1e30
MASK_DIST = 1e30
LANE = 128

RAW_D = 768
RAW_PASS = 2176
RAW_W = RAW_D + RAW_PASS
Z_QD, Z_KD, Z_VD = 0, 512, 1024
Z_QA, Z_KA, Z_VA = 1280, 1536, 1792
Z_QB, Z_VB = 2048, 2304
Z_QC, Z_KC, Z_VC = 2560, 2816, 3072
Z_KB = 3328
Z_PAD = 3456
Z_W = 3584

TM_PROJ = 512
TQ_DIFF = 256
TQ_MLA = 512
TK_DENSE = 256
N_STREAMS = 4
V_ROWS = HEAD_DIM + 16
LOG2E = 1.4426950408889634
FOLD_LO = 64.0
FOLD_RANGE = 16384
AUG_ROWS = 16
BAND_BLK = 128
BAND_UNROLL = 8
BAND_MIN_BLOCKS = 8
VMEM_LIMIT = 56 * 1024 * 1024


def _cparams(sem):
    return pltpu.CompilerParams(dimension_semantics=sem, vmem_limit_bytes=VMEM_LIMIT)


def _rms(x, g):
    ms = jnp.mean(x * x, axis=-1, keepdims=True)
    return x * lax.rsqrt(ms + EPS) * g


def _dot(a, b):
    return jnp.dot(a, b, preferred_element_type=F32)


def _dot_nt(a, b):
    return lax.dot_general(a, b, (((1,), (1,)), ((), ())), preferred_element_type=F32)


def _lane_iota(w):
    return lax.broadcasted_iota(jnp.int32, (1, w), 1)


def _lane_mask(w, lo, hi):
    lane = _lane_iota(w)
    return (lane >= lo) & (lane < hi)


def _ada_kernel(c_ref, w_ref, b_ref, o_ref):
    c = c_ref[...]
    c_act = c / (1.0 + jnp.exp(-c))
    o_ref[0] = jnp.dot(c_act, w_ref[0], precision=lax.Precision.HIGHEST,
                       preferred_element_type=F32) + b_ref[0]


def _ada_modulation(c, w_ada, b_ada):
    n_layers, d, six_d = w_ada.shape
    bsz = c.shape[0]
    tn = 1024
    return pl.pallas_call(
        _ada_kernel,
        grid=(n_layers, six_d // tn),
        in_specs=[pl.BlockSpec((bsz, d), lambda l, j: (0, 0)),
                  pl.BlockSpec((1, d, tn), lambda l, j: (l, 0, j)),
                  pl.BlockSpec((1, 1, tn), lambda l, j: (l, 0, j))],
        out_specs=pl.BlockSpec((1, bsz, tn), lambda l, j: (l, 0, j)),
        out_shape=jax.ShapeDtypeStruct((n_layers, bsz, six_d), F32),
        compiler_params=_cparams(("arbitrary", "arbitrary")),
        name="ada_modulation",
    )(c, w_ada, b_ada.reshape(n_layers, 1, six_d))


def _rope_kernel(p_ref, inv_ref, cos_ref, sin_ref):
    ang = p_ref[0] * inv_ref[...]
    cos_ref[0] = jnp.cos(ang)
    sin_ref[0] = jnp.sin(ang)


def _rope_tables(pos_col, inv_full):
    bsz, t, _ = pos_col.shape
    tm = 1024
    spec = pl.BlockSpec((1, tm, LANE), lambda b, i: (b, i, 0))
    return pl.pallas_call(
        _rope_kernel,
        grid=(bsz, t // tm),
        in_specs=[pl.BlockSpec((1, tm, 1), lambda b, i: (b, i, 0)),
                  pl.BlockSpec((1, LANE), lambda b, i: (0, 0))],
        out_specs=[spec, spec],
        out_shape=[jax.ShapeDtypeStruct((bsz, t, LANE), F32)] * 2,
        compiler_params=_cparams(("arbitrary", "arbitrary")),
        name="rope_tables",
    )(pos_col, inv_full)


def _inproj_kernel(x_ref, mod_ref, g_ref, w1_ref, cs_ref, gq_ref, gkv_ref, wq2_ref, wkv2_ref,
                   cos_ref, sin_ref, z_ref, zf4_ref, zf16_ref, fold_ref, *, d_scale):
    x = x_ref[0]
    tm = x.shape[0]
    mod = mod_ref[0]
    shift, scale = mod[:, 0:D_MODEL], mod[:, D_MODEL:2 * D_MODEL]
    h = (_rms(x, g_ref[...]) * (1.0 + scale) + shift).astype(BF16)
    raw = _dot(h, w1_ref[...])
    cq = raw[:, 0:D_Q_RANK]
    ckv = raw[:, D_Q_RANK:D_Q_RANK + D_KV_RANK]
    kr_a = raw[:, 512:640]
    kr_b = raw[:, 640:768]
    cos = cos_ref[0]
    sin = sin_ref[0]
    cos4 = jnp.concatenate([cos] * N_HEADS, axis=1)
    sin4 = jnp.concatenate([sin] * N_HEADS, axis=1)
    q2 = _dot(_rms(cq, gq_ref[...]).astype(BF16), wq2_ref[...])
    qd = (q2[:, 0:512] * cos4 + q2[:, 512:1024] * sin4) * d_scale
    kv2 = _dot(_rms(ckv, gkv_ref[...]).astype(BF16), wkv2_ref[...])
    kr = kr_a * cos + kr_b * sin
    kd = kv2[:, 0:512] + jnp.concatenate([kr] * N_HEADS, axis=1)
    z_ref[0, :, Z_QD:Z_QD + 512] = qd.astype(BF16)
    z_ref[0, :, Z_KD:Z_KD + 512] = kd.astype(BF16)
    z_ref[0, :, Z_VD:Z_VD + 256] = kv2[:, 512:768].astype(BF16)
    passed = raw[:, RAW_D:RAW_W] * cs_ref[...]
    z_ref[0, :, Z_QA:Z_PAD] = passed.astype(BF16)
    z_ref[0, :, Z_PAD:Z_W] = jnp.zeros((tm, Z_W - Z_PAD), BF16)
    n_slab = 3 * GROUP_W // LANE
    for s in range(n_slab):
        fold_ref[s] = passed[:, s * LANE:(s + 1) * LANE]
    for dil, out_ref in ((DILATIONS[1], zf4_ref), (DILATIONS[2], zf16_ref)):
        for r in range(dil):
            for s in range(n_slab):
                out_ref[0, r, :, s * LANE:(s + 1) * LANE] = (
                    fold_ref[s, pl.ds(r, tm // dil, stride=dil), :].astype(BF16))


def _input_projection(x, mod, g_pre, w1, col_scale, g_q, g_kv, wq2, wkv2, cos_t, sin_t, layer):
    bsz, t, d = x.shape
    tm = TM_PROJ
    const = lambda b, i: (0, 0)
    lsel = lambda b, i: (layer, 0, 0)
    kern = functools.partial(_inproj_kernel, d_scale=(HEAD_DIM + D_ROPE) ** -0.5 * LOG2E)
    return pl.pallas_call(
        kern,
        grid=(bsz, t // tm),
        in_specs=[pl.BlockSpec((1, tm, d), lambda b, i: (b, i, 0)),
                  pl.BlockSpec((None, 1, 1, 6 * d), lambda b, i: (layer, b, 0, 0)),
                  pl.BlockSpec((None, 1, d), lsel),
                  pl.BlockSpec((None, d, RAW_W), lsel),
                  pl.BlockSpec((1, RAW_PASS), const),
                  pl.BlockSpec((None, 1, D_Q_RANK), lsel),
                  pl.BlockSpec((None, 1, D_KV_RANK), lsel),
                  pl.BlockSpec((None, D_Q_RANK, 1024), lsel),
                  pl.BlockSpec((None, D_KV_RANK, 768), lsel),
                  pl.BlockSpec((1, tm, LANE), lambda b, i: (b, i, 0)),
                  pl.BlockSpec((1, tm, LANE), lambda b, i: (b, i, 0))],
        out_specs=[pl.BlockSpec((1, tm, Z_W), lambda b, i: (b, i, 0))] +
                  [pl.BlockSpec((1, dil, tm // dil, 3 * GROUP_W), lambda b, i: (b, 0, i, 0))
                   for dil in DILATIONS[1:]],
        out_shape=[jax.ShapeDtypeStruct((bsz, t, Z_W), BF16)] +
                  [jax.ShapeDtypeStruct((bsz, dil, t // dil, 3 * GROUP_W), BF16) for dil in DILATIONS[1:]],
        scratch_shapes=[pltpu.VMEM((3 * GROUP_W // LANE, tm, LANE), F32)],
        compiler_params=_cparams(("arbitrary", "arbitrary")),
        name="input_projection",
    )(x, mod, g_pre, w1, col_scale, g_q, g_kv, wq2, wkv2, cos_t, sin_t)


def _banded_kernel(sl_ref, sink_ref, q_ref, k_ref, v_ref, pk_ref, pq_ref, *rest,
                   n, dil, radius, groups, with_lse):
    if with_lse:
        o_ref, lse_ref, vm_ref = rest
    else:
        o_ref, vm_ref = rest
        lse_ref = None
    blk = BAND_BLK
    win = min(blk + 2 * radius, n)
    n_g = len(groups)
    n_res = q_ref.shape[0]
    lane_o = _lane_iota(GROUP_W)
    for rl in range(n_res):
        v_all = v_ref[rl]
        for g, grp in enumerate(groups):
            vmask = (lane_o >= grp[4]) & (lane_o < grp[5])
            vm_ref[rl, g] = jnp.where(vmask, v_all, jnp.zeros_like(v_all))
    stat_row = lax.broadcasted_iota(jnp.int32, (LANE, 1), 0)

    def one_block(rl, i):
        r0 = pl.multiple_of(i * blk, blk)
        start = pl.multiple_of(jnp.clip(r0 - radius, 0, n - win), radius)
        q = q_ref[rl, pl.ds(r0, blk), :].astype(F32)
        q_t = {}
        parts = []
        for (q_lo, q_w, qm_lo, qm_hi, _, _, _, _) in groups:
            if q_lo not in q_t:
                q_t[q_lo] = q[:, q_lo:q_lo + q_w].T
            dim = lax.broadcasted_iota(jnp.int32, (q_w, 1), 0)
            parts.append(jnp.where((dim >= qm_lo) & (dim < qm_hi), q_t[q_lo], 0.0))
        qs_t = jnp.concatenate(parts, axis=1).astype(BF16)
        s_t = _dot(k_ref[rl, pl.ds(start, win), :], qs_t)
        dist = jnp.abs(pk_ref[rl, pl.ds(start, win), :] - pq_ref[rl, i])
        ki = start + lax.broadcasted_iota(jnp.int32, (win, 1), 0)
        qi = r0 + lax.broadcasted_iota(jnp.int32, (1, blk), 1)
        dist = jnp.where(jnp.abs(qi - ki) <= radius, dist, MASK_DIST)
        p_parts = []
        inv_rows = jnp.zeros((LANE, blk), F32)
        lse_rows = jnp.zeros((LANE, blk), F32)
        for g, grp in enumerate(groups):
            sg = s_t[:, g * blk:(g + 1) * blk] - (sl_ref[grp[6]] * LOG2E) * dist
            m = jnp.max(sg, axis=0, keepdims=True)
            if grp[7] is not None:
                sk = sink_ref[grp[7]] * LOG2E
                m = jnp.maximum(m, sk)
            p = jnp.exp2(sg - m)
            den = jnp.sum(p, axis=0, keepdims=True)
            if grp[7] is not None:
                den = den + jnp.exp2(sk - m)
            p_parts.append(p.T.astype(BF16))
            inv_rows = jnp.where(stat_row == g, 1.0 / den, inv_rows)
            lse_rows = jnp.where(stat_row == g, m + jnp.log2(den), lse_rows)
        pcat = jnp.concatenate(p_parts, axis=1)
        vcat = jnp.concatenate([vm_ref[rl, g, pl.ds(start, win), :] for g in range(n_g)], axis=0)
        o = _dot(pcat, vcat)
        inv_cols = inv_rows.T
        lse_cols = lse_rows.T
        inv_full = jnp.zeros((blk, GROUP_W), F32)
        lse_full = jnp.zeros((blk, GROUP_W), F32)
        for g, grp in enumerate(groups):
            vmask = (lane_o >= grp[4]) & (lane_o < grp[5])
            inv_full = jnp.where(vmask, inv_cols[:, g:g + 1], inv_full)
            lse_full = jnp.where(vmask, lse_cols[:, g:g + 1], lse_full)
        o = o * inv_full
        if not with_lse:
            o_ref[pl.ds(r0, blk), :] = o.astype(o_ref.dtype)
            return
        if dil == 1:
            rows = pl.ds(r0, blk)
        else:
            rows = pl.ds(r0 * dil + pl.program_id(1) * n_res + rl, blk, stride=dil)
        for s in range(GROUP_W // LANE):
            o_ref[s, rows, :] = o[:, s * LANE:(s + 1) * LANE]
            lse_ref[s, rows, :] = lse_full[:, s * LANE:(s + 1) * LANE]

    unroll = min(BAND_UNROLL, n // blk)
    trips = n // (blk * unroll)
    for rl in range(n_res):
        def body(j, carry, rl=rl):
            for u in range(unroll):
                one_block(rl, j * unroll + u)
            return carry

        if trips == 1:
            body(0, 0)
        else:
            lax.fori_loop(0, trips, body, 0)


def _banded_attention(slopes, sink, src, pos, *, dil, radius, groups, q_col, q_w, k_col, k_w, v_col,
                      with_lse, name):
    bsz, t = pos.shape
    n = t // dil
    blk = BAND_BLK
    n_res = max(1, min(dil, BAND_MIN_BLOCKS // (n // blk)))
    if dil == 1:
        src = src.reshape(bsz, 1, t, src.shape[-1])
    pos_f = pos.reshape(bsz, n, dil).transpose(0, 2, 1)
    pos_k = jnp.broadcast_to(pos_f[..., None], (bsz, dil, n, LANE))
    pos_q = pos_f.reshape(bsz, dil, n // blk, 1, blk)
    kern = functools.partial(_banded_kernel, n=n, dil=dil, radius=radius, groups=groups, with_lse=with_lse)
    if with_lse:
        o_spec = pl.BlockSpec((None, GROUP_W // LANE, t, LANE), lambda b, r: (b, 0, 0, 0))
        out_specs = [o_spec, o_spec]
        out_shape = [jax.ShapeDtypeStruct((bsz, GROUP_W // LANE, t, LANE), F32)] * 2
    else:
        out_specs = [pl.BlockSpec((None, t, GROUP_W), lambda b, r: (b, 0, 0))]
        out_shape = [jax.ShapeDtypeStruct((bsz, t, GROUP_W), BF16)]
    smem = pl.BlockSpec(memory_space=pltpu.SMEM)

    def zspec(col, w):
        return pl.BlockSpec((None, n_res, n, w), lambda b, r: (b, r, 0, col // w))

    return pl.pallas_call(
        kern,
        grid=(bsz, dil // n_res),
        in_specs=[smem, smem, zspec(q_col, q_w), zspec(k_col, k_w), zspec(v_col, GROUP_W),
                  pl.BlockSpec((None, n_res, n, LANE), lambda b, r: (b, r, 0, 0)),
                  pl.BlockSpec((None, n_res, n // blk, 1, blk), lambda b, r: (b, r, 0, 0, 0))],
        out_specs=out_specs,
        out_shape=out_shape,
        scratch_shapes=[pltpu.VMEM((n_res, len(groups), n, GROUP_W), BF16)],
        compiler_params=_cparams(("arbitrary", "arbitrary")),
        name=name,
    )(slopes, sink, src, src, src, pos_k, pos_q)


def _cat_lanes(xs):
    return xs[0] if len(xs) == 1 else jnp.concatenate(xs, axis=1)


def _flash_loop(n_kb, streams, vs_fn, n_maps, tq, tk):
    n = n_maps * tq
    n_g = 2 * n_maps
    hd = HEAD_DIM
    pv_w = 2 * LANE
    n_pv = 2 * (n // pv_w)

    def scores_part(tile_fn, s_ref, g):
        s = tile_fn()
        s_ref[:, g * tq:(g + 1) * tq] = s
        return jnp.max(s, axis=0, keepdims=True)

    def softmax_part(s_ref, p_ref, m, col_max, g):
        mn = jnp.maximum(m[g], col_max[g])
        h, mp = divmod(g, n_maps)
        p_ref[h * tk:(h + 1) * tk, mp * tq:(mp + 1) * tq] = (
            jnp.exp2(s_ref[:, g * tq:(g + 1) * tq] - mn).astype(BF16))
        return mn, jnp.exp2(m[g] - mn)

    def pv_part(kb, p_ref, acc_ref, alphas, piece):
        h, j = divmod(piece, n // pv_w)
        c = slice(j * pv_w, (j + 1) * pv_w)
        pv = _dot(vs_fn(kb, h), p_ref[h * tk:(h + 1) * tk, c])
        acc_ref[h, :, c] = alphas[h][:, c] * acc_ref[h, :, c] + pv

    def pack_alphas(alphas):
        return _cat_lanes(alphas[0:n_maps]), _cat_lanes(alphas[n_maps:n_g])

    def scores_stage(kb, stream):
        scores_fn, fix_fn, s_refs, _, _ = stream
        tiles = scores_fn(kb)
        col_max = [scores_part(tiles[g], s_refs[0], g) for g in range(n_g)]
        return col_max if fix_fn is None else fix_fn(kb, s_refs[0], col_max)

    def step(kb, cur, state):
        nxt = 1 - cur
        kb_next = jnp.minimum(kb + 1, n_kb - 1)
        kb_prev = jnp.maximum(kb - 1, 0)
        tiles = [stream[0](kb_next) for stream in streams]
        new = [([], [], []) for _ in streams]
        for g in range(n_g):
            for (_, _, s_refs, p_refs, acc_ref), (m, col_max, a_prev), (m_new, alphas, col_max_next), tl in zip(
                    streams, state, new, tiles):
                mn, alpha = softmax_part(s_refs[cur], p_refs[cur], m, col_max, g)
                m_new.append(mn)
                alphas.append(alpha)
                col_max_next.append(scores_part(tl[g], s_refs[nxt], g))
                for piece in range(g * n_pv // n_g, (g + 1) * n_pv // n_g):
                    pv_part(kb_prev, p_refs[nxt], acc_ref, a_prev, piece)
        out = []
        for (_, fix_fn, s_refs, _, _), (m_new, alphas, col_max_next) in zip(streams, new):
            if fix_fn is not None:
                col_max_next = fix_fn(kb_next, s_refs[nxt], col_max_next)
            out.append((m_new, col_max_next, pack_alphas(alphas)))
        return out

    def body(i, state):
        return step(2 * i + 1, 1, step(2 * i, 0, state))

    init = []
    for stream in streams:
        stream[4][...] = jnp.zeros((2, V_ROWS, n), F32)
        stream[3][1][...] = jnp.zeros((2 * tk, n), BF16)
        init.append(([jnp.full((1, tq), NEG, F32)] * n_g, scores_stage(0, stream),
                     (jnp.ones((1, n), F32), jnp.ones((1, n), F32))))
    state = lax.fori_loop(0, n_kb // 2, body, init)
    results = []
    for (_, _, _, p_refs, acc_ref), (_, _, a_last) in zip(streams, state):
        for piece in range(n_pv):
            pv_part(n_kb - 1, p_refs[1], acc_ref, a_last, piece)
        results.append((acc_ref[0, 0:hd], acc_ref[1, 0:hd], acc_ref[0, hd:hd + 1], acc_ref[1, hd:hd + 1]))
    return results


def _flash_scratch(n_maps, tq, tk):
    n = n_maps * tq
    return ([pltpu.VMEM((tk, 2 * n), F32)] * 2 + [pltpu.VMEM((2 * tk, n), BF16)] * 2
            + [pltpu.VMEM((2, V_ROWS, n), F32)])


def _fill_vt(v_ref, vt_ref, t, tk):
    v_t = v_ref[0].astype(F32).T
    row = lax.broadcasted_iota(jnp.int32, (V_ROWS - HEAD_DIM, tk), 0)
    ones = jnp.where(row == 0, 1.0, 0.0)
    for kb in range(t // tk):
        for h in range(2):
            vals = v_t[h * HEAD_DIM:(h + 1) * HEAD_DIM, kb * tk:(kb + 1) * tk]
            vt_ref[h, kb] = jnp.concatenate([vals, ones], axis=0).astype(BF16)


def _split3(x):
    x1 = x.astype(BF16).astype(F32)
    x2 = (x - x1).astype(BF16).astype(F32)
    x3 = (x - x1 - x2).astype(BF16).astype(F32)
    return x1, x2, x3


def _diff_kernel(sl_ref, base_ref, qlo_ref, qhi_ref, klo_ref, khi_ref, lq1_ref, lk1_ref, lq2_ref, lk2_ref,
                 gd_ref, q_ref, k_ref, v_ref, pq_ref, pk_ref, o_ref, vt_ref, kx_ref, *flash_scratch,
                 t, lam_init):
    tq, tk = TQ_DIFF, TK_DENSE
    n_q, n_k = t // tq, t // tk
    b = pl.program_id(0)
    hp = pl.program_id(1)
    qi = pl.program_id(2)
    base = base_ref[b]
    base_f = base.astype(F32)
    lane = _lane_iota(LANE)

    @pl.when(qi == 0)
    def _():
        _fill_vt(v_ref, vt_ref, t, tk)
        p_rel = pk_ref[0] - base_f
        p_hi = jnp.floor(p_rel * (1.0 / FOLD_LO)) * FOLD_LO
        p_lo = p_rel - p_hi
        aug = jnp.where(lane < 3, p_hi, jnp.where(lane < 6, p_lo, jnp.where(lane < 9, 1.0, 0.0)))
        kx_ref[:, 0:LANE] = k_ref[0]
        kx_ref[:, LANE:2 * LANE] = aug.astype(BF16)

    slopes2 = [sl_ref[N_HEADS + 2 * hp] * LOG2E, sl_ref[N_HEADS + 2 * hp + 1] * LOG2E]
    row = lax.broadcasted_iota(jnp.int32, (LANE, 1), 0)
    arow = lax.broadcasted_iota(jnp.int32, (AUG_ROWS, 1), 0)
    pad_rows = jnp.zeros((LANE - AUG_ROWS, 4 * tq), BF16)

    def make_stream(st, scratch):
        q_t = q_ref[0, st * tq:(st + 1) * tq, :].astype(F32).T
        qs_t = jnp.concatenate(
            [jnp.where((row >= C_QK * g) & (row < C_QK * (g + 1)), q_t, 0.0) for g in range(4)],
            axis=1).astype(BF16)
        pq = pq_ref[0, st]
        aug_parts = []
        for h in range(2):
            m1, m2, m3 = _split3(jnp.full((1, tq), slopes2[h], F32))
            w1, w2, w3 = _split3(-slopes2[h] * (pq - base_f))
            blk = jnp.where((arow == 0) | (arow == 3), m1,
                  jnp.where((arow == 1) | (arow == 4), m2,
                  jnp.where((arow == 2) | (arow == 5), m3,
                  jnp.where(arow == 6, w1, jnp.where(arow == 7, w2, jnp.where(arow == 8, w3, 0.0))))))
            aug_parts += [blk, blk]
        aug_plus = jnp.concatenate(aug_parts, axis=1)
        q_idx = b * n_q + N_STREAMS * qi + st
        q_hi = qhi_ref[q_idx]
        q_lo = qlo_ref[q_idx]
        q_ok = q_hi - base < FOLD_RANGE

        def sign_of(kb):
            k_lo = klo_ref[b * n_k + kb]
            k_hi = khi_ref[b * n_k + kb]
            ok = q_ok & (k_hi - base < FOLD_RANGE)
            return jnp.where(ok & (k_hi <= q_lo), 1, jnp.where(ok & (k_lo >= q_hi), -1, 0))

        def scores(kb):
            off = pl.multiple_of(kb * tk, tk)
            kk = kx_ref[pl.ds(off, tk), :]
            sg = sign_of(kb).astype(F32)
            rhs = jnp.concatenate([qs_t, (sg * aug_plus).astype(BF16), pad_rows], axis=0)
            return [functools.partial(lambda g: _dot(kk, rhs[:, g * tq:(g + 1) * tq]), g)
                    for g in range(4)]

        def fix(kb, s_ref, col_max):
            def explicit():
                off = pl.multiple_of(kb * tk, tk)
                pk = pk_ref[0, pl.ds(off, tk), :]
                dist = jnp.abs(jnp.concatenate([pk] * (tq // LANE), axis=1) - pq)
                out = []
                for g in range(4):
                    s = s_ref[:, g * tq:(g + 1) * tq] - slopes2[g // 2] * dist
                    s_ref[:, g * tq:(g + 1) * tq] = s
                    out.append(jnp.max(s, axis=0, keepdims=True))
                return out
            return lax.cond(sign_of(kb) == 0, explicit, lambda: col_max)

        return (scores, fix, scratch[0:2], scratch[2:4], scratch[4])

    streams = [make_stream(st, flash_scratch[5 * st:5 * st + 5]) for st in range(N_STREAMS)]
    results = _flash_loop(n_k, streams, lambda kb, h: vt_ref[h, kb], 2, tq, tk)
    lam = (jnp.exp(jnp.sum(lq1_ref[...] * lk1_ref[...], axis=1, keepdims=True))
           - jnp.exp(jnp.sum(lq2_ref[...] * lk2_ref[...], axis=1, keepdims=True)) + lam_init)
    for st, (acc0, acc1, l0, l1) in enumerate(results):
        o0 = acc0 / l0
        o1 = acc1 / l1
        o_t = jnp.concatenate([o0[:, 0:tq] - lam * o0[:, tq:2 * tq],
                               o1[:, 0:tq] - lam * o1[:, tq:2 * tq]], axis=0)
        o = o_t.T
        o_sq = o * o
        ms0 = jnp.sum(jnp.where(lane < HEAD_DIM, o_sq, 0.0), axis=1, keepdims=True) * (1.0 / HEAD_DIM)
        ms1 = jnp.sum(jnp.where(lane >= HEAD_DIM, o_sq, 0.0), axis=1, keepdims=True) * (1.0 / HEAD_DIM)
        ms = jnp.where(lane < HEAD_DIM, ms0, ms1)
        y = o * lax.rsqrt(ms + EPS) * gd_ref[...]
        o_ref[0, st * tq:(st + 1) * tq, :] = (y * (1.0 - lam_init)).astype(o_ref.dtype)


def _block_stats(positions):
    bsz, t = positions.shape
    pq = positions.reshape(bsz, t // TQ_DIFF, TQ_DIFF)
    pk = positions.reshape(bsz, t // TK_DENSE, TK_DENSE)
    return (jnp.min(positions, axis=1), jnp.min(pq, axis=2).reshape(-1), jnp.max(pq, axis=2).reshape(-1),
            jnp.min(pk, axis=2).reshape(-1), jnp.max(pk, axis=2).reshape(-1))


def _diff_attention(slopes, stats, lq1, lk1, lq2, lk2, g_diff2, z, pos_q_rows, pos_k_lanes, layer):
    bsz, t, _ = z.shape
    tq, tk = TQ_DIFF, TK_DENSE
    lam_init = 0.8 - 0.6 * math.exp(-0.3 * layer)
    kern = functools.partial(_diff_kernel, t=t, lam_init=lam_init)
    lsel = lambda b, h, i: (layer, 0, 0)
    smem = pl.BlockSpec(memory_space=pltpu.SMEM)
    return pl.pallas_call(
        kern,
        grid=(bsz, 2, t // (N_STREAMS * tq)),
        in_specs=[smem] * 6 +
                 [pl.BlockSpec((None, 1, C_QK), lsel), pl.BlockSpec((None, 1, C_QK), lsel),
                  pl.BlockSpec((None, 1, C_QK), lsel), pl.BlockSpec((None, 1, C_QK), lsel),
                  pl.BlockSpec((None, 1, LANE), lsel),
                  pl.BlockSpec((1, N_STREAMS * tq, LANE), lambda b, h, i: (b, i, Z_QC // LANE + h)),
                  pl.BlockSpec((1, t, LANE), lambda b, h, i: (b, 0, Z_KC // LANE + h)),
                  pl.BlockSpec((1, t, LANE), lambda b, h, i: (b, 0, Z_VC // LANE + h)),
                  pl.BlockSpec((1, N_STREAMS, 1, tq), lambda b, h, i: (b, i, 0, 0)),
                  pl.BlockSpec((1, t, LANE), lambda b, h, i: (b, 0, 0))],
        out_specs=pl.BlockSpec((1, N_STREAMS * tq, LANE), lambda b, h, i: (b, i, h)),
        out_shape=jax.ShapeDtypeStruct((bsz, t, GROUP_W), BF16),
        scratch_shapes=[pltpu.VMEM((2, t // tk, V_ROWS, tk), BF16), pltpu.VMEM((t, 2 * LANE), BF16)]
                       + _flash_scratch(2, tq, tk) * N_STREAMS,
        compiler_params=_cparams(("arbitrary", "arbitrary", "arbitrary")),
        name="diff_attention",
    )(slopes, *stats, lq1, lk1, lq2, lk2, g_diff2, z, z, z, pos_q_rows, pos_k_lanes)


def _mla_kernel(q_ref, k_ref, v_ref, o_ref, vt_ref, *flash_scratch, t):
    tq, tk = TQ_MLA, TK_DENSE

    @pl.when(pl.program_id(2) == 0)
    def _():
        _fill_vt(v_ref, vt_ref, t, tk)

    def make_stream(st, scratch):
        q = q_ref[0, st * tq:(st + 1) * tq, :].astype(F32)
        q_t = [q[:, h * LANE:(h + 1) * LANE].T.astype(BF16) for h in range(2)]

        def scores(kb):
            off = pl.multiple_of(kb * tk, tk)
            kk = k_ref[0, pl.ds(off, tk), :]
            return [functools.partial(lambda h: _dot(kk[:, h * LANE:(h + 1) * LANE], q_t[h]), h)
                    for h in range(2)]

        return (scores, None, scratch[0:2], scratch[2:4], scratch[4])

    streams = [make_stream(st, flash_scratch[5 * st:5 * st + 5]) for st in range(N_STREAMS)]
    results = _flash_loop(t // tk, streams, lambda kb, h: vt_ref[h, kb], 1, tq, tk)
    for st, (acc0, acc1, l0, l1) in enumerate(results):
        o_t = jnp.concatenate([acc0 / l0, acc1 / l1], axis=0)
        o_ref[0, st * tq:(st + 1) * tq, :] = o_t.T.astype(o_ref.dtype)


def _mla_attention(z):
    bsz, t, _ = z.shape
    tq, tk = TQ_MLA, TK_DENSE
    kern = functools.partial(_mla_kernel, t=t)
    return pl.pallas_call(
        kern,
        grid=(bsz, 2, t // (N_STREAMS * tq)),
        in_specs=[pl.BlockSpec((1, N_STREAMS * tq, 2 * LANE), lambda b, h, i: (b, i, Z_QD // 256 + h)),
                  pl.BlockSpec((1, t, 2 * LANE), lambda b, h, i: (b, 0, Z_KD // 256 + h)),
                  pl.BlockSpec((1, t, LANE), lambda b, h, i: (b, 0, Z_VD // LANE + h))],
        out_specs=pl.BlockSpec((1, N_STREAMS * tq, LANE), lambda b, h, i: (b, i, h)),
        out_shape=jax.ShapeDtypeStruct((bsz, t, GROUP_W), BF16),
        scratch_shapes=[pltpu.VMEM((2, t // tk, V_ROWS, tk), BF16)] + _flash_scratch(1, tq, tk) * N_STREAMS,
        compiler_params=_cparams(("arbitrary", "arbitrary", "arbitrary")),
        name="mla_attention",
    )(z, z, z)


def _mix_ffn_kernel(x_ref, oa1_ref, oa2_ref, oa3_ref, la1_ref, la2_ref, la3_ref, ob_ref, oc_ref, od_ref,
                    mod_ref, gpm_ref, gpf_ref, gqf_ref, wout_ref, wgu_ref, wd_ref, out_ref, a_ref):
    d = D_MODEL
    x = x_ref[0]
    mod = mod_ref[0]
    gate_m = mod[:, 2 * d:3 * d]
    shift_f, scale_f, gate_f = mod[:, 3 * d:4 * d], mod[:, 4 * d:5 * d], mod[:, 5 * d:6 * d]
    slabs = lambda ref: jnp.concatenate([ref[0, s] for s in range(GROUP_W // LANE)], axis=1)
    l1, l2, l3 = slabs(la1_ref), slabs(la2_ref), slabs(la3_ref)
    mx = jnp.maximum(jnp.maximum(l1, l2), l3)
    e1, e2, e3 = jnp.exp2(l1 - mx), jnp.exp2(l2 - mx), jnp.exp2(l3 - mx)
    mix_a = (e1 * slabs(oa1_ref) + e2 * slabs(oa2_ref) + e3 * slabs(oa3_ref)) / (e1 + e2 + e3)
    mix = jnp.concatenate([mix_a.astype(BF16), ob_ref[0], oc_ref[0], od_ref[0]], axis=1)
    y = _dot(mix, wout_ref[...])
    x1 = x + gate_m * _rms(y, gpm_ref[...])
    h = (_rms(x1, gpf_ref[...]) * (1.0 + scale_f) + shift_f).astype(BF16)
    ch = 256
    for c0 in range(0, D_FF, ch):
        g = _dot(h, wgu_ref[:, c0:c0 + ch])
        u = _dot(h, wgu_ref[:, D_FF + c0:D_FF + c0 + ch])
        a_ref[:, c0:c0 + ch] = (g / (1.0 + jnp.exp(-g)) * u).astype(BF16)
    y2 = _dot(a_ref[...], wd_ref[...])
    out_ref[0] = x1 + gate_f * _rms(y2, gqf_ref[...])


def _mix_ffn(x, oa, la, ob, oc, od, mod, g_post_mix, g_pre_ffn, g_post_ffn, w_out, w_gu, w_down, layer):
    bsz, t, d = x.shape
    tm = TM_PROJ
    tok = lambda w: pl.BlockSpec((1, tm, w), lambda b, i: (b, i, 0))
    lsel = lambda b, i: (layer, 0, 0)
    once = dict(pipeline_mode=pl.Buffered(1))
    return pl.pallas_call(
        _mix_ffn_kernel,
        grid=(bsz, t // tm),
        in_specs=[tok(d)] +
                 [pl.BlockSpec((1, GROUP_W // LANE, tm, LANE), lambda b, i: (b, 0, i, 0))] * 6 +
                 [tok(GROUP_W)] * 3 +
                 [pl.BlockSpec((None, 1, 1, 6 * d), lambda b, i: (layer, b, 0, 0)),
                  pl.BlockSpec((None, 1, d), lsel), pl.BlockSpec((None, 1, d), lsel),
                  pl.BlockSpec((None, 1, d), lsel),
                  pl.BlockSpec((None, d, d), lsel, **once),
                  pl.BlockSpec((None, d, 2 * D_FF), lsel, **once),
                  pl.BlockSpec((None, D_FF, d), lsel, **once)],
        out_specs=tok(d),
        out_shape=jax.ShapeDtypeStruct((bsz, t, d), F32),
        scratch_shapes=[pltpu.VMEM((tm, D_FF), BF16)],
        compiler_params=_cparams(("arbitrary", "arbitrary")),
        name="mix_ffn",
    )(x, *oa, *la, ob, oc, od, mod, g_post_mix, g_pre_ffn, g_post_ffn, w_out, w_gu, w_down)


def _rot_cols(w):
    half = w.shape[-1] // 2
    return jnp.concatenate([-w[..., half:], w[..., :half]], axis=-1)


def _prep_w_in(w_in):
    n_layers, d, _ = w_in.shape
    z64 = jnp.zeros((n_layers, d, 64), w_in.dtype)
    z32 = jnp.zeros((n_layers, d, 32), w_in.dtype)
    col = lambda a, b: w_in[:, :, a:b]
    kr = col(2560, 2592)
    qb = 768
    vb = 1152
    parts = [col(2048, 2432), col(2432, 2560),
             z64, kr, z32,
             z64, _rot_cols(kr), z32,
             col(0, 768),
             col(qb, qb + 64), col(qb + 128, qb + 192), col(qb + 64, qb + 128), col(qb + 192, qb + 256),
             col(vb, vb + 64), col(vb, vb + 64), col(vb + 64, vb + 128), col(vb + 64, vb + 128),
             col(1280, 2048),
             col(1024, 1152)]
    return jnp.concatenate(parts, axis=-1).astype(BF16)


def _prep_w_uq(w_uq):
    n_layers, r, _ = w_uq.shape
    z64 = jnp.zeros((n_layers, r, 64), w_uq.dtype)
    z32 = jnp.zeros((n_layers, r, 32), w_uq.dtype)
    main, rot = [], []
    for h in range(N_HEADS):
        base = h * (HEAD_DIM + D_ROPE)
        nope = w_uq[:, :, base:base + HEAD_DIM]
        rope = w_uq[:, :, base + HEAD_DIM:base + HEAD_DIM + D_ROPE]
        main += [nope, rope, z32]
        rot += [z64, _rot_cols(rope), z32]
    return jnp.concatenate(main + rot, axis=-1).astype(BF16)


def _prep_w_ukv(w_ukv):
    n_layers, r, _ = w_ukv.shape
    z64 = jnp.zeros((n_layers, r, 64), w_ukv.dtype)
    ks, vs = [], []
    for h in range(N_HEADS):
        base = h * 2 * HEAD_DIM
        ks += [w_ukv[:, :, base:base + HEAD_DIM], z64]
        vs.append(w_ukv[:, :, base + HEAD_DIM:base + 2 * HEAD_DIM])
    return jnp.concatenate(ks + vs, axis=-1).astype(BF16)


def _col_scale():
    cs = np.ones((1, RAW_PASS), np.float32)
    cs[0, Z_QA - Z_QA:Z_QA - Z_QA + 256] = HEAD_DIM ** -0.5 * LOG2E
    cs[0, Z_QB - Z_QA:Z_QB - Z_QA + 256] = HEAD_DIM ** -0.5 * LOG2E
    cs[0, Z_QC - Z_QA:Z_QC - Z_QA + 256] = C_QK ** -0.5 * LOG2E
    return jnp.asarray(cs)


_A_GROUPS = tuple((0, GROUP_W, 64 * h, 64 * h + 64, 64 * h, 64 * h + 64, 8 + h, None) for h in range(4))
_B_GROUPS = tuple((128 * r, 128, 64 * g, 64 * g + 64, 128 * g + 64 * r, 128 * g + 64 * r + 64, 2 * g + r, 2 * g + r)
                  for g in range(2) for r in range(2))


def kernel(x, c, positions, w_ada, b_ada, g_pre_mix, g_post_mix, w_in, sink_logits, lam_q1, lam_k1, lam_q2, lam_k2, g_diff, g_mla_q, g_mla_kv, w_uq, w_ukv, w_out, g_pre_ffn, g_post_ffn, w_gate_up, w_down):
    bsz, t, d = x.shape
    n_layers = w_in.shape[0]
    pos = positions.astype(F32)
    pos_col = pos.reshape(bsz, t, 1)
    pos_q_rows = pos.reshape(bsz, t // TQ_DIFF, 1, TQ_DIFF)
    pos_k_lanes = jnp.broadcast_to(pos[:, :, None], (bsz, t, LANE))
    stats = _block_stats(positions)
    j = jnp.arange(1, N_ALIBI + 1, dtype=F32)
    slopes = jnp.exp2(-8.0 * j / N_ALIBI)
    half = D_ROPE // 2
    inv = jnp.power(ROPE_THETA, -jnp.arange(half, dtype=F32) / half)
    inv_full = jnp.concatenate([jnp.zeros((64,), F32), inv, inv, jnp.zeros((32,), F32)]).reshape(1, LANE)

    w1 = _prep_w_in(w_in)
    wq2 = _prep_w_uq(w_uq)
    wkv2 = _prep_w_ukv(w_ukv)
    w_out_b = w_out.astype(BF16)
    w_gu_b = w_gate_up.astype(BF16)
    w_down_b = w_down.astype(BF16)
    col_scale = _col_scale()
    r3 = lambda a: a.reshape(n_layers, 1, a.shape[-1])
    g_diff2 = r3(jnp.concatenate([g_diff, g_diff], axis=-1))
    no_sink = jnp.zeros((1,), F32)

    mod = _ada_modulation(c, w_ada, b_ada).reshape(n_layers, bsz, 1, 6 * d)
    cos_t, sin_t = _rope_tables(pos_col, inv_full)

    for layer in range(n_layers):
        z, zf4, zf16 = _input_projection(x, mod, r3(g_pre_mix), w1, col_scale, r3(g_mla_q), r3(g_mla_kv),
                                         wq2, wkv2, cos_t, sin_t, layer)
        oa, la = [], []
        for dil, src, col0 in zip(DILATIONS, (z, zf4, zf16), (Z_QA, 0, 0)):
            o_i, l_i = _banded_attention(slopes, no_sink, src, pos, dil=dil, radius=A_RADIUS, groups=_A_GROUPS,
                                         q_col=col0, q_w=GROUP_W, k_col=col0 + GROUP_W, k_w=GROUP_W,
                                         v_col=col0 + 2 * GROUP_W,
                                         with_lse=True, name="dilated_attention_%d" % dil)
            oa.append(o_i)
            la.append(l_i)
        (ob,) = _banded_attention(slopes, sink_logits[layer], z, pos, dil=1, radius=B_RADIUS, groups=_B_GROUPS,
                                  q_col=Z_QB, q_w=GROUP_W, k_col=Z_KB, k_w=LANE, v_col=Z_VB,
                                  with_lse=False, name="windowed_attention")
        oc = _diff_attention(slopes, stats, r3(lam_q1), r3(lam_k1), r3(lam_q2), r3(lam_k2), g_diff2, z,
                             pos_q_rows, pos_k_lanes, layer)
        od = _mla_attention(z)
        x = _mix_ffn(x, oa, la, ob, oc, od, mod, r3(g_post_mix), r3(g_pre_ffn), r3(g_post_ffn),
                     w_out_b, w_gu_b, w_down_b, layer)
    return x
```
